```python
import math, functools
import jax, jax.numpy as jnp
from jax import lax
import numpy as np

D_MODEL = 1024
BATCH = 16
SEQ = 4096
DEPTH = 2

HEAD_DIM = 64
NSA_Q_HEADS = 8
NSA_KV_HEADS = 2
NSA_GROUP = NSA_Q_HEADS // NSA_KV_HEADS
CMP_LEN = 32
CMP_STRIDE = 16
CMP_HIDDEN = 128
SEL_LEN = 64
SEL_TOP_N = 16
WINDOW = 512
ATTN_Q_BLOCK = 128
SEL_Q_BLOCK = 32
N_BRANCHES = 3
GMLP_GROUPS = 8
GMLP_HEAD_DIM = 64
GMLP_CHUNK = 128
CONV_WIDTH = 3
FFN_HIDDEN = -(-8 * D_MODEL // (3 * 256)) * 256

Q_DIM = NSA_Q_HEADS * HEAD_DIM
KV_DIM = NSA_KV_HEADS * HEAD_DIM
GATE_DIM = NSA_Q_HEADS * N_BRANCHES
GMLP_DIM = GMLP_GROUPS * GMLP_HEAD_DIM
IN0_DIM = Q_DIM + 6 * KV_DIM + GATE_DIM + 2 * GMLP_DIM
MIX0_DIM = Q_DIM + GMLP_DIM

ALPHA = (2 * DEPTH) ** 0.25
BETA = (8 * DEPTH) ** -0.25
N_EVEN = (DEPTH + 1) // 2
N_ODD = DEPTH // 2
LN_EPS = 1e-5
NEG_INF = -1e30
FORCE_SCORE = 1e4

kernel_name = "hybrid_nsa_gmlp_shortconv_deepnorm_adaln"


def layer_norm(x, g, b):
    xf = x.astype(jnp.float32)
    mu = jnp.mean(xf, axis=-1, keepdims=True)
    var = jnp.mean(jnp.square(xf - mu), axis=-1, keepdims=True)
    y = (xf - mu) * lax.rsqrt(var + LN_EPS)
    return (y * g.astype(jnp.float32) + b.astype(jnp.float32)).astype(x.dtype)


def masked_softmax(s, mask):
    s = jnp.where(mask, s.astype(jnp.float32), NEG_INF)
    return jax.nn.softmax(s, axis=-1) * mask


def ada_modulation(c, w, b):
    m = jax.nn.silu(c) @ w + b
    shift, scale, gate = jnp.split(m[:, None, :], 3, axis=-1)
    return shift, scale, gate


def residual_update(x, c, ada_w, ada_b, ln_g, ln_b, sublayer):
    shift, scale, gate = ada_modulation(c, ada_w, ada_b)
    out = sublayer(x * (1 + scale) + shift)
    return layer_norm(ALPHA * x + (1 + gate) * out, ln_g, ln_b)


def compress_blocks(kv, pos, w1, w2):
    b_, g_, s_, dk = kv.shape
    r = CMP_LEN // CMP_STRIDE
    ch = kv.reshape(b_, g_, s_ // CMP_STRIDE, CMP_STRIDE, dk)
    n_c = s_ // CMP_STRIDE - r + 1
    blk = jnp.concatenate([ch[:, :, i:i + n_c] for i in range(r)], axis=3)
    blk = (blk + pos).reshape(b_, g_, n_c, CMP_LEN * dk)
    return jax.nn.gelu(blk @ w1) @ w2


def nsa_attention(q, k_c, v_c, k_s, v_s, k_w, v_w, gates, cmp_pos, cmp_w1, cmp_w2):
    b_, g_, r_, s_, dk = q.shape
    q = q * dk ** -0.5
    kc = compress_blocks(k_c, cmp_pos[0], cmp_w1[0], cmp_w2[0])
    vc = compress_blocks(v_c, cmp_pos[1], cmp_w1[1], cmp_w2[1])
    n_cmp = kc.shape[2]
    n_sel = s_ // SEL_LEN
    top_n = min(SEL_TOP_N, n_sel)
    cmp_end = jnp.arange(n_cmp) * CMP_STRIDE + CMP_LEN - 1
    kk = jnp.arange(n_cmp)[:, None]
    jj = jnp.arange(n_sel)[None, :]
    cover = ((kk * CMP_STRIDE < (jj + 1) * SEL_LEN)
             & (kk * CMP_STRIDE + CMP_LEN > jj * SEL_LEN)).astype(jnp.float32)

    def cmp_block(i):
        q_blk = lax.dynamic_slice_in_dim(q, i * ATTN_Q_BLOCK, ATTN_Q_BLOCK, axis=3)
        qpos = i * ATTN_Q_BLOCK + jnp.arange(ATTN_Q_BLOCK)
        s = jnp.einsum('bgrqd,bgkd->bgrqk', q_blk, kc)
        mask = cmp_end[None, :] <= qpos[:, None]
        p = masked_softmax(s, mask)
        o = jnp.einsum('bgrqk,bgkd->bgrqd', p.astype(vc.dtype), vc)
        imp = jnp.einsum('bgqk,kj->bgqj', p.sum(axis=2), cover)
        cur = (qpos // SEL_LEN)[:, None]
        valid = jj * SEL_LEN <= qpos[:, None]
        forced = (jj == 0) | (jj == cur) | (jj == cur - 1)
        imp = jnp.where(forced, FORCE_SCORE, jnp.where(valid, imp, -FORCE_SCORE))
        _, idx = lax.top_k(imp, top_n)
        return o, idx.astype(jnp.int32)

    o_cmp, sel_idx = lax.map(cmp_block, jnp.arange(s_ // ATTN_Q_BLOCK))
    o_cmp = jnp.moveaxis(o_cmp, 0, 3).reshape(b_, g_, r_, s_, dk)
    sel_idx = jnp.moveaxis(sel_idx, 0, 2).reshape(b_, g_, s_, top_n)

    ks_blocks = k_s.reshape(b_, g_, n_sel, SEL_LEN, dk)
    vs_blocks = v_s.reshape(b_, g_, n_sel, SEL_LEN, dk)
    bi = jnp.arange(b_)[:, None, None, None]
    gi = jnp.arange(g_)[None, :, None, None]

    def sel_block(i):
        q_blk = lax.dynamic_slice_in_dim(q, i * SEL_Q_BLOCK, SEL_Q_BLOCK, axis=3)
        idx = lax.dynamic_slice_in_dim(sel_idx, i * SEL_Q_BLOCK, SEL_Q_BLOCK, axis=2)
        qpos = i * SEL_Q_BLOCK + jnp.arange(SEL_Q_BLOCK)
        kg = ks_blocks[bi, gi, idx].reshape(b_, g_, SEL_Q_BLOCK, top_n * SEL_LEN, dk)
        vg = vs_blocks[bi, gi, idx].reshape(b_, g_, SEL_Q_BLOCK, top_n * SEL_LEN, dk)
        kpos = (idx[..., None] * SEL_LEN + jnp.arange(SEL_LEN)).reshape(
            b_, g_, SEL_Q_BLOCK, top_n * SEL_LEN)
        mask = (kpos <= qpos[:, None])[:, :, None]
        s = jnp.einsum('bgrqd,bgqkd->bgrqk', q_blk, kg)
        p = masked_softmax(s, mask)
        return jnp.einsum('bgrqk,bgqkd->bgrqd', p.astype(vg.dtype), vg)

    o_sel = lax.map(sel_block, jnp.arange(s_ // SEL_Q_BLOCK))
    o_sel = jnp.moveaxis(o_sel, 0, 3).reshape(b_, g_, r_, s_, dk)

    kw_pad = jnp.pad(k_w, ((0, 0), (0, 0), (WINDOW, 0), (0, 0)))
    vw_pad = jnp.pad(v_w, ((0, 0), (0, 0), (WINDOW, 0), (0, 0)))
    span = ATTN_Q_BLOCK + WINDOW

    def win_block(i):
        start = i * ATTN_Q_BLOCK
        q_blk = lax.dynamic_slice_in_dim(q, start, ATTN_Q_BLOCK, axis=3)
        kb = lax.dynamic_slice_in_dim(kw_pad, start, span, axis=2)
        vb = lax.dynamic_slice_in_dim(vw_pad, start, span, axis=2)
        qpos = start + jnp.arange(ATTN_Q_BLOCK)
        kpos = start - WINDOW + jnp.arange(span)
        dist = qpos[:, None] - kpos[None, :]
        mask = (dist >= 0) & (dist < WINDOW) & (kpos >= 0)[None, :]
        s = jnp.einsum('bgrqd,bgkd->bgrqk', q_blk, kb)
        p = masked_softmax(s, mask)
        return jnp.einsum('bgrqk,bgkd->bgrqd', p.astype(vb.dtype), vb)

    o_win = lax.map(win_block, jnp.arange(s_ // ATTN_Q_BLOCK))
    o_win = jnp.moveaxis(o_win, 0, 3).reshape(b_, g_, r_, s_, dk)

    return gates[..., 0:1] * o_cmp + gates[..., 1:2] * o_sel + gates[..., 2:3] * o_win


def chunked_gmlp(u, v, norm_g, w_s, b_s):
    b_, s_, gm, dm = u.shape
    u = jax.nn.gelu(u)
    vf = jax.nn.gelu(v).astype(jnp.float32)
    mu = jnp.mean(vf, axis=-1, keepdims=True)
    var = jnp.mean(jnp.square(vf - mu), axis=-1, keepdims=True)
    v = ((vf - mu) * lax.rsqrt(var + LN_EPS) * norm_g.astype(jnp.float32)).astype(u.dtype)
    v = v.reshape(b_, s_ // GMLP_CHUNK, GMLP_CHUNK, gm, dm)
    causal = jnp.tril(jnp.ones((GMLP_CHUNK, GMLP_CHUNK), dtype=bool))
    w = jnp.where(causal, w_s, jnp.zeros_like(w_s))
    mixed = jnp.einsum('gts,bnsgd->bntgd', w, v) + b_s.T[:, :, None]
    return (u * mixed.reshape(b_, s_, gm, dm)).reshape(b_, s_, gm * dm)


def hybrid_mixer(h, w_in, cmp_pos, cmp_w1, cmp_w2, gmlp_norm_g, gmlp_ws, gmlp_bs, w_out):
    b_, s_, _ = h.shape
    sizes = [Q_DIM] + [KV_DIM] * 6 + [GATE_DIM, GMLP_DIM, GMLP_DIM]
    offsets = np.cumsum(sizes)[:-1].tolist()
    q, kc, vc, ks, vs, kw, vw, g, u, v = jnp.split(h @ w_in, offsets, axis=-1)
    q = q.reshape(b_, s_, NSA_KV_HEADS, NSA_GROUP, HEAD_DIM).transpose(0, 2, 3, 1, 4)
    kv = [t.reshape(b_, s_, NSA_KV_HEADS, HEAD_DIM).transpose(0, 2, 1, 3)
          for t in (kc, vc, ks, vs, kw, vw)]
    gates = jax.nn.sigmoid(g).reshape(b_, s_, NSA_KV_HEADS, NSA_GROUP, N_BRANCHES)
    gates = gates.transpose(0, 2, 3, 1, 4)
    o_nsa = nsa_attention(q, *kv, gates, cmp_pos, cmp_w1, cmp_w2)
    o_nsa = o_nsa.transpose(0, 3, 1, 2, 4).reshape(b_, s_, Q_DIM)
    o_gmlp = chunked_gmlp(u.reshape(b_, s_, GMLP_GROUPS, GMLP_HEAD_DIM),
                          v.reshape(b_, s_, GMLP_GROUPS, GMLP_HEAD_DIM),
                          gmlp_norm_g, gmlp_ws, gmlp_bs)
    return jnp.concatenate([o_nsa, o_gmlp], axis=-1) @ w_out


def short_conv_mixer(h, w_in, conv_w, w_out):
    b_gate, c_gate, z = jnp.split(h @ w_in, 3, axis=-1)
    y = lax.conv_general_dilated(
        c_gate * z, conv_w[:, None, :], window_strides=(1,),
        padding=[(CONV_WIDTH - 1, 0)], dimension_numbers=('NWC', 'WIO', 'NWC'),
        feature_group_count=conv_w.shape[-1])
    return (b_gate * y) @ w_out


def swiglu(h, w_in, w_out):
    gate, up = jnp.split(h @ w_in, 2, axis=-1)
    return (jax.nn.silu(gate) * up) @ w_out


def setup_inputs(seed: int = 0) -> dict:
    key = jax.random.key(seed)
    ks = jax.random.split(key, 24)
    f32 = jnp.float32
    nrm = lambda k, shape, s: jax.random.normal(k, shape, f32) * s
    D = D_MODEL
    return {
        "x": nrm(ks[0], (BATCH, SEQ, D), 1.0),
        "c": nrm(ks[1], (BATCH, D), 1.0),
        "ada_w": nrm(ks[2], (DEPTH, 2, D, 3 * D), 0.1 * D ** -0.5),
        "ada_b": nrm(ks[3], (DEPTH, 2, 3 * D), 0.01),
        "ln_g": 1.0 + nrm(ks[4], (DEPTH, 2, D), 0.01),
        "ln_b": nrm(ks[5], (DEPTH, 2, D), 0.01),
        "even_w_in": nrm(ks[6], (N_EVEN, D, IN0_DIM), D ** -0.5),
        "even_cmp_pos": nrm(ks[7], (N_EVEN, 2, CMP_LEN, HEAD_DIM), 0.02),
        "even_cmp_w1": nrm(ks[8], (N_EVEN, 2, CMP_LEN * HEAD_DIM, CMP_HIDDEN), (CMP_LEN * HEAD_DIM) ** -0.5),
        "even_cmp_w2": nrm(ks[9], (N_EVEN, 2, CMP_HIDDEN, HEAD_DIM), CMP_HIDDEN ** -0.5),
        "even_gmlp_norm_g": 1.0 + nrm(ks[10], (N_EVEN, GMLP_GROUPS, GMLP_HEAD_DIM), 0.01),
        "even_gmlp_ws": nrm(ks[11], (N_EVEN, GMLP_GROUPS, GMLP_CHUNK, GMLP_CHUNK), GMLP_CHUNK ** -0.5),
        "even_gmlp_bs": 1.0 + nrm(ks[12], (N_EVEN, GMLP_GROUPS, GMLP_CHUNK), 0.01),
        "even_w_out": nrm(ks[13], (N_EVEN, MIX0_DIM, D), BETA * MIX0_DIM ** -0.5),
        "odd_w_in": nrm(ks[14], (N_ODD, D, 3 * D), D ** -0.5),
        "odd_conv_w": nrm(ks[15], (N_ODD, CONV_WIDTH, D), CONV_WIDTH ** -0.5),
        "odd_w_out": nrm(ks[16], (N_ODD, D, D), BETA * D ** -0.5),
        "ffn_w_in": nrm(ks[17], (DEPTH, D, 2 * FFN_HIDDEN), D ** -0.5),
        "ffn_w_out": nrm(ks[18], (DEPTH, FFN_HIDDEN, D), BETA * FFN_HIDDEN ** -0.5),
    }


def reference(x, c, ada_w, ada_b, ln_g, ln_b, even_w_in, even_cmp_pos, even_cmp_w1,
              even_cmp_w2, even_gmlp_norm_g, even_gmlp_ws, even_gmlp_bs, even_w_out,
              odd_w_in, odd_conv_w, odd_w_out, ffn_w_in, ffn_w_out):
    for layer in range(DEPTH):
        j = layer // 2
        if layer % 2 == 0:
            mixer = functools.partial(
                hybrid_mixer, w_in=even_w_in[j], cmp_pos=even_cmp_pos[j],
                cmp_w1=even_cmp_w1[j], cmp_w2=even_cmp_w2[j],
                gmlp_norm_g=even_gmlp_norm_g[j], gmlp_ws=even_gmlp_ws[j],
                gmlp_bs=even_gmlp_bs[j], w_out=even_w_out[j])
        else:
            mixer = functools.partial(
                short_conv_mixer, w_in=odd_w_in[j], conv_w=odd_conv_w[j], w_out=odd_w_out[j])
        x = residual_update(x, c, ada_w[layer, 0], ada_b[layer, 0],
                            ln_g[layer, 0], ln_b[layer, 0], mixer)
        ffn = functools.partial(swiglu, w_in=ffn_w_in[layer], w_out=ffn_w_out[layer])
        x = residual_update(x, c, ada_w[layer, 1], ada_b[layer, 1],
                            ln_g[layer, 1], ln_b[layer, 1], ffn)
    return x
```

```python
import functools

import numpy as np
import jax
import jax.numpy as jnp
from jax import lax
from jax.experimental import pallas as pl
from jax.experimental.pallas import tpu as pltpu

HEAD_DIM = 64
NSA_Q_HEADS = 8
NSA_KV_HEADS = 2
NSA_GROUP = NSA_Q_HEADS // NSA_KV_HEADS
CMP_LEN = 32
CMP_STRIDE = 16
CMP_HIDDEN = 128
SEL_LEN = 64
SEL_TOP_N = 16
WINDOW = 512
N_BRANCHES = 3
GMLP_GROUPS = 8
GMLP_HEAD_DIM = 64
GMLP_CHUNK = 128
CONV_WIDTH = 3
DEPTH = 2
ALPHA = (2 * DEPTH) ** 0.25
LN_EPS = 1e-5
NEG_INF = -1e30
FORCE_SCORE = 1e4

LANES = 128
HALF = LANES // 2
HALF_SHIFT = 6
SEL_SHIFT = 6
Q_TILE = 128
KV_CHUNK = 512
ROW_TILE = 512
FFN_CHUNK = 256
VMEM_LIMIT = 56 * 1024 * 1024

F32 = jnp.float32
BF16 = jnp.bfloat16


def _dot(a, b):
    return jnp.dot(a, b, preferred_element_type=F32)


def _dot_nt(a, b):
    return lax.dot_general(a, b, (((1,), (1,)), ((), ())), preferred_element_type=F32)


def _layer_norm(v, g, b):
    mu = jnp.mean(v, axis=-1, keepdims=True)
    d = v - mu
    var = jnp.mean(d * d, axis=-1, keepdims=True)
    return d * lax.rsqrt(var + LN_EPS) * g + b


def _const_spec(shape):
    zeros = (0,) * len(shape)
    return pl.BlockSpec(shape, lambda *_: zeros, pipeline_mode=pl.Buffered(1))


def _params(n_axes, vmem=None):
    return pltpu.CompilerParams(dimension_semantics=("arbitrary",) * n_axes, vmem_limit_bytes=vmem)


def _ada_kernel(c_ref, w_ref, b_ref, o_ref):
    a = jax.nn.silu(c_ref[...]).astype(BF16)
    o_ref[0] = _dot(a, w_ref[0].astype(BF16)) + b_ref[0]


def _ada_modulation(c, ada_w, ada_b):
    b_, d = c.shape
    n_pairs = ada_w.shape[0] * ada_w.shape[1]
    w = ada_w.reshape(n_pairs, d, 3 * d)
    bias = ada_b.reshape(n_pairs, 1, 3 * d)
    mod = pl.pallas_call(
        _ada_kernel,
        grid=(n_pairs, 3),
        in_specs=[
            pl.BlockSpec((b_, d), lambda i, j: (0, 0)),
            pl.BlockSpec((1, d, d), lambda i, j: (i, 0, j)),
            pl.BlockSpec((1, 1, d), lambda i, j: (i, 0, j)),
        ],
        out_specs=pl.BlockSpec((1, b_, d), lambda i, j: (i, 0, j)),
        out_shape=jax.ShapeDtypeStruct((n_pairs, b_, 3 * d), F32),
        compiler_params=_params(2),
        name="ada",
    )(c, w, bias)
    return mod.reshape(n_pairs * b_ * 3, 1, d)


def _mod_spec(pair, which, b_, d):
    return pl.BlockSpec((1, 1, d), lambda b, i: ((pair * b_ + b) * 3 + which, 0, 0))


def _segment_sum(x, seg):
    hi = x.astype(BF16)
    lo = (x - hi.astype(F32)).astype(BF16)
    return _dot(hi, seg) + _dot(lo, seg)


def _inproj0_kernel(x_ref, shift_ref, scale_ref, wq_ref, wkv_ref, wu_ref, wv_ref, seg_ref, ng_ref,
                    wsp_ref, bsp_ref,
                    q_ref, kc_ref, vc_ref, ks_ref, vs_ref, kw_ref, vw_ref, gate_ref, og_ref):
    tm = x_ref.shape[1]
    h = (x_ref[0] * (1.0 + scale_ref[0]) + shift_ref[0]).astype(BF16)
    q_ref[0] = _dot(h, wq_ref[...]) * (HEAD_DIM ** -0.5)
    kv = _dot(h, wkv_ref[...])
    for n, ref in enumerate((kc_ref, vc_ref, ks_ref, vs_ref, kw_ref, vw_ref)):
        ref[0] = kv[:, n * LANES:(n + 1) * LANES].astype(ref.dtype)
    gate_ref[0] = jax.nn.sigmoid(kv[:, 6 * LANES:7 * LANES])

    u = jax.nn.gelu(_dot(h, wu_ref[...]))
    vf = jax.nn.gelu(_dot(h, wv_ref[...]))
    seg = seg_ref[...]
    inv = 1.0 / GMLP_HEAD_DIM
    mu = _segment_sum(vf, seg) * inv
    dv = vf - mu
    var = _segment_sum(dv * dv, seg) * inv
    vn = dv * lax.rsqrt(var + LN_EPS) * ng_ref[...]

    left = lax.broadcasted_iota(jnp.int32, (GMLP_CHUNK, LANES), 1) < HALF
    t_idx = lax.broadcasted_iota(jnp.int32, (GMLP_CHUNK, 2 * GMLP_CHUNK), 0)
    s_idx = lax.broadcasted_iota(jnp.int32, (GMLP_CHUNK, 2 * GMLP_CHUNK), 1) & (GMLP_CHUNK - 1)
    causal = s_idx <= t_idx
    for p in range(GMLP_GROUPS // 2):
        cols = slice(p * LANES, (p + 1) * LANES)
        wcat = jnp.where(causal, wsp_ref[p], 0.0).astype(BF16)
        bias = bsp_ref[:, cols]
        for c in range(tm // GMLP_CHUNK):
            rows = slice(c * GMLP_CHUNK, (c + 1) * GMLP_CHUNK)
            vp = vn[rows, cols]
            v2 = jnp.concatenate([jnp.where(left, vp, 0.0), jnp.where(left, 0.0, vp)], axis=0).astype(BF16)
            mixed = _dot(wcat, v2)
            og_ref[0, rows, cols] = (u[rows, cols] * (mixed + bias)).astype(og_ref.dtype)


def _inproj0(x, mod, w_in, norm_g, w_s, b_s):
    b_, s_, d = x.shape
    tm = ROW_TILE
    qd = NSA_Q_HEADS * HEAD_DIM
    kvd = NSA_KV_HEADS * HEAD_DIM
    gd = NSA_Q_HEADS * N_BRANCHES
    gm = GMLP_GROUPS * GMLP_HEAD_DIM
    o_kv = qd
    o_g = qd + 6 * kvd
    o_u = o_g + gd
    o_v = o_u + gm
    wq = w_in[:, :qd].astype(BF16)
    wkv = jnp.concatenate([w_in[:, o_kv:o_g], w_in[:, o_g:o_u],
                           jnp.zeros((d, LANES - gd), w_in.dtype)], axis=1).astype(BF16)
    wu = w_in[:, o_u:o_v].astype(BF16)
    wv = w_in[:, o_v:o_v + gm].astype(BF16)
    seg = jnp.asarray(np.kron(np.eye(GMLP_GROUPS), np.ones((GMLP_HEAD_DIM, GMLP_HEAD_DIM))), BF16)
    ng = norm_g.reshape(1, gm)
    wsp = w_s.reshape(GMLP_GROUPS // 2, 2, GMLP_CHUNK, GMLP_CHUNK).transpose(0, 2, 1, 3).reshape(
        GMLP_GROUPS // 2, GMLP_CHUNK, 2 * GMLP_CHUNK)
    bsp = jnp.broadcast_to(b_s.T[:, :, None], (GMLP_CHUNK, GMLP_GROUPS, GMLP_HEAD_DIM)).reshape(GMLP_CHUNK, gm)

    tok = lambda w: pl.BlockSpec((1, tm, w), lambda b, i: (b, i, 0))
    shp = lambda w, dt: jax.ShapeDtypeStruct((b_, s_, w), dt)
    return pl.pallas_call(
        _inproj0_kernel,
        grid=(b_, s_ // tm),
        in_specs=[tok(d), _mod_spec(0, 0, b_, d), _mod_spec(0, 1, b_, d),
                  _const_spec(wq.shape), _const_spec(wkv.shape), _const_spec(wu.shape), _const_spec(wv.shape),
                  _const_spec(seg.shape), _const_spec(ng.shape), _const_spec(wsp.shape), _const_spec(bsp.shape)],
        out_specs=[tok(qd)] + [tok(LANES)] * 7 + [tok(gm)],
        out_shape=[shp(qd, F32), shp(LANES, F32), shp(LANES, F32)] + [shp(LANES, BF16)] * 4
                  + [shp(LANES, F32), shp(gm, BF16)],
        compiler_params=_params(2, VMEM_LIMIT),
        name="inproj0",
    )(x, mod, mod, wq, wkv, wu, wv, seg, ng, wsp, bsp)


def _compress_kernel(kch_ref, vch_ref, pos_ref, w1a_ref, w1b_ref, w2_ref, kco_ref, vco_ref):
    for t, (src, dst) in enumerate(((kch_ref, kco_ref), (vch_ref, vco_ref))):
        ch = src[0]
        n = ch.shape[0]
        first = (ch + pos_ref[t, 0:1]).astype(BF16)
        second = (ch + pos_ref[t, 1:2]).astype(BF16)
        ha = _dot(first, w1a_ref[t])
        hb = _dot(second, w1b_ref[t])
        h1 = ha + pltpu.roll(hb, n - 1, axis=0)
        dst[0] = _dot(jax.nn.gelu(h1).astype(BF16), w2_ref[t]).astype(dst.dtype)


def _compress(kc, vc, cmp_pos, cmp_w1, cmp_w2):
    b_, s_, _ = kc.shape
    nch = s_ // CMP_STRIDE
    r = CMP_LEN // CMP_STRIDE
    assert r == 2
    eye = jnp.eye(NSA_KV_HEADS, dtype=cmp_w1.dtype)
    w1 = cmp_w1.reshape(2, r, CMP_STRIDE, HEAD_DIM, CMP_HIDDEN)
    w1 = jnp.einsum("thldc,gk->thlgdkc", w1, eye).reshape(
        2, r, CMP_STRIDE * LANES, NSA_KV_HEADS * CMP_HIDDEN).astype(BF16)
    w2 = jnp.einsum("tcd,gk->tgckd", cmp_w2, eye).reshape(2, NSA_KV_HEADS * CMP_HIDDEN, LANES).astype(BF16)
    pos = jnp.broadcast_to(cmp_pos.reshape(2, r, CMP_STRIDE, 1, HEAD_DIM),
                           (2, r, CMP_STRIDE, NSA_KV_HEADS, HEAD_DIM)).reshape(2, r, CMP_STRIDE * LANES)
    kch = kc.reshape(b_, nch, CMP_STRIDE * LANES)
    vch = vc.reshape(b_, nch, CMP_STRIDE * LANES)
    ch_spec = pl.BlockSpec((1, nch, CMP_STRIDE * LANES), lambda b: (b, 0, 0))
    out_spec = pl.BlockSpec((1, nch, LANES), lambda b: (b, 0, 0))
    out_shape = jax.ShapeDtypeStruct((b_, nch, LANES), BF16)
    return pl.pallas_call(
        _compress_kernel,
        grid=(b_,),
        in_specs=[ch_spec, ch_spec, _const_spec(pos.shape), _const_spec(w1[:, 0].shape),
                  _const_spec(w1[:, 1].shape), _const_spec(w2.shape)],
        out_specs=[out_spec, out_spec],
        out_shape=[out_shape, out_shape],
        compiler_params=_params(1, VMEM_LIMIT),
        name="compress",
    )(kch, vch, pos, w1[:, 0], w1[:, 1], w2)


def _masked_softmax(s, mask):
    sm = jnp.where(mask, s, NEG_INF)
    e = jnp.exp(sm - jnp.max(sm, axis=1, keepdims=True))
    return jnp.where(mask, e * (1.0 / jnp.sum(e, axis=1, keepdims=True)), 0.0)


def _online_softmax_step(s, v, m, l, acc):
    m_new = jnp.maximum(m, jnp.max(s, axis=1, keepdims=True))
    alpha = jnp.exp(m - m_new)
    p = jnp.exp(s - m_new)
    l = alpha * l + jnp.sum(p, axis=1, keepdims=True)
    acc = alpha * acc + _dot(p.astype(BF16), v)
    return m_new, l, acc


def _nsa_kernel(q_ref, gate_ref, kc_ref, vc_ref, ks_ref, vs_ref, kw_ref, vw_ref, cover_ref, o_ref, kaug_ref):
    i = pl.program_id(1)
    start = i * Q_TILE
    s_len = ks_ref.shape[1]
    n_sel = s_len // SEL_LEN
    n_cmp = kc_ref.shape[1]
    rows = NSA_GROUP * Q_TILE
    lane = lax.broadcasted_iota(jnp.int32, (1, LANES), 1)

    @pl.when(i == 0)
    def _():
        def build(c, carry):
            off = pl.multiple_of(c * KV_CHUNK, KV_CHUNK)
            key_blk = (off + lax.broadcasted_iota(jnp.int32, (KV_CHUNK, LANES), 0)) >> SEL_SHIFT
            ln = lax.broadcasted_iota(jnp.int32, (KV_CHUNK, LANES), 1)
            ks = ks_ref[0, pl.ds(off, KV_CHUNK), :].astype(F32)
            for g in range(NSA_KV_HEADS):
                onehot = (ln - (1 - g) * HALF == key_blk).astype(F32)
                kaug_ref[g, pl.ds(off, KV_CHUNK), :] = jnp.where((ln >> HALF_SHIFT) == g, ks, onehot).astype(BF16)
            return carry
        lax.fori_loop(0, s_len // KV_CHUNK, build, 0)

    q = q_ref[0]
    gates = gate_ref[0]
    qpos = start + (lax.broadcasted_iota(jnp.int32, (rows, 1), 0) & (Q_TILE - 1))
    tpos = start + lax.broadcasted_iota(jnp.int32, (Q_TILE, 1), 0)

    for g in range(NSA_KV_HEADS):
        in_g = (lane >> HALF_SHIFT) == g
        heads = []
        for r in range(NSA_GROUP):
            hh = NSA_GROUP * g + r
            qc = q[:, (hh // 2) * LANES:(hh // 2 + 1) * LANES]
            heads.append(pltpu.roll(qc, HALF, axis=1) if r % 2 != g else qc)
        qs = jnp.concatenate(heads, axis=0)
        q0 = jnp.where(in_g, qs, 0.0).astype(BF16)

        s = _dot_nt(q0, kc_ref[0])
        cmp_end = lax.broadcasted_iota(jnp.int32, (1, n_cmp), 1) * CMP_STRIDE + (CMP_LEN - 1)
        p = _masked_softmax(s, cmp_end <= qpos)
        o_cmp = _dot(p.astype(BF16), vc_ref[0])
        p_sum = p[0:Q_TILE]
        for r in range(1, NSA_GROUP):
            p_sum = p_sum + p[r * Q_TILE:(r + 1) * Q_TILE]
        imp = jnp.dot(p_sum, cover_ref[...], precision=lax.Precision.HIGHEST,
                      preferred_element_type=F32)

        cur = tpos >> SEL_SHIFT
        forced = (lane == 0) | (lane == cur) | (lane == cur - 1)
        score = jnp.where(forced, FORCE_SCORE, jnp.where(lane * SEL_LEN <= tpos, imp, -FORCE_SCORE))
        score_t = score.T[0:n_sel]
        j_idx = lax.broadcasted_iota(jnp.int32, (n_sel, 1), 0)
        rank = jnp.zeros((n_sel, Q_TILE), jnp.int32)
        for jp in range(n_sel):
            other = score_t[jp:jp + 1, :]
            beats = (other > score_t) | ((other == score_t) & (jp < j_idx))
            rank = rank + beats.astype(jnp.int32)
        bias_t = jnp.where(rank < min(SEL_TOP_N, n_sel), 0.0, NEG_INF)
        lead, tail = HALF * (1 - g), LANES - n_sel - HALF * (1 - g)
        parts = [jnp.zeros((n, Q_TILE), F32) if n else None for n in (lead, tail)]
        parts = [v for v in (parts[0], bias_t, parts[1]) if v is not None]
        bias = jnp.concatenate(parts, axis=0).T
        qa = jnp.where(in_g, qs, jnp.concatenate([bias] * NSA_GROUP, axis=0)).astype(BF16)

        diag = lax.div(i, KV_CHUNK // Q_TILE)

        def sel_step(c, carry, qa=qa, g=g):
            off = pl.multiple_of(c * KV_CHUNK, KV_CHUNK)
            sc = _dot_nt(qa, kaug_ref[g, pl.ds(off, KV_CHUNK), :])
            return _online_softmax_step(sc, vs_ref[0, pl.ds(off, KV_CHUNK), :], *carry)

        init = (jnp.full((rows, 1), NEG_INF, F32), jnp.zeros((rows, 1), F32), jnp.zeros((rows, LANES), F32))
        m, l, acc = lax.fori_loop(0, diag, sel_step, init)
        off = pl.multiple_of(diag * KV_CHUNK, KV_CHUNK)
        sc = _dot_nt(qa, kaug_ref[g, pl.ds(off, KV_CHUNK), :])
        kpos = off + lax.broadcasted_iota(jnp.int32, (1, KV_CHUNK), 1)
        sc = jnp.where(kpos <= qpos, sc, NEG_INF)
        m, l, acc = _online_softmax_step(sc, vs_ref[0, pl.ds(off, KV_CHUNK), :], m, l, acc)
        o_sel = acc * (1.0 / l)

        span = WINDOW + Q_TILE
        ws = pl.multiple_of(jnp.maximum(start - WINDOW, 0), Q_TILE)
        sw = _dot_nt(q0, kw_ref[0, pl.ds(ws, span), :])
        dist = qpos - (ws + lax.broadcasted_iota(jnp.int32, (1, span), 1))
        pw = _masked_softmax(sw, (dist >= 0) & (dist < WINDOW))
        o_win = _dot(pw.astype(BF16), vw_ref[0, pl.ds(ws, span), :])

        for rp in range(NSA_GROUP // 2):
            pair = []
            for r in (2 * rp, 2 * rp + 1):
                sl = slice(r * Q_TILE, (r + 1) * Q_TILE)
                c0 = (NSA_GROUP * g + r) * N_BRANCHES
                o = (gates[:, c0:c0 + 1] * o_cmp[sl] + gates[:, c0 + 1:c0 + 2] * o_sel[sl]
                     + gates[:, c0 + 2:c0 + 3] * o_win[sl])
                pair.append(pltpu.roll(o, HALF, axis=1) if r % 2 != g else o)
            col = (NSA_GROUP // 2) * g + rp
            o_ref[0, :, col * LANES:(col + 1) * LANES] = jnp.where(lane < HALF, pair[0], pair[1]).astype(o_ref.dtype)


def _cover_matrix(n_cmp_padded, n_sel):
    kk = np.arange(n_cmp_padded)[:, None]
    jj = np.arange(LANES)[None, :]
    cover = ((kk * CMP_STRIDE < (jj + 1) * SEL_LEN) & (kk * CMP_STRIDE + CMP_LEN > jj * SEL_LEN)
             & (kk < n_cmp_padded - 1) & (jj < n_sel))
    return jnp.asarray(cover, F32)


def _nsa(q, gates, kcc, vcc, ks, vs, kw, vw):
    b_, s_, qd = q.shape
    n_cmp = kcc.shape[1]
    n_sel = s_ // SEL_LEN
    assert n_sel <= HALF and s_ % KV_CHUNK == 0 and s_ >= WINDOW + Q_TILE
    cover = _cover_matrix(n_cmp, n_sel)
    tok = lambda w: pl.BlockSpec((1, Q_TILE, w), lambda b, i: (b, i, 0))
    seq = lambda n: pl.BlockSpec((1, n, LANES), lambda b, i: (b, 0, 0))
    return pl.pallas_call(
        _nsa_kernel,
        grid=(b_, s_ // Q_TILE),
        in_specs=[tok(qd), tok(LANES), seq(n_cmp), seq(n_cmp), seq(s_), seq(s_), seq(s_), seq(s_),
                  _const_spec(cover.shape)],
        out_specs=tok(qd),
        out_shape=jax.ShapeDtypeStruct((b_, s_, qd), BF16),
        scratch_shapes=[pltpu.VMEM((NSA_KV_HEADS, s_, LANES), BF16)],
        compiler_params=_params(2, VMEM_LIMIT),
        name="nsa",
    )(q, gates, kcc, vcc, ks, vs, kw, vw, cover)


def _swiglu_block(y, shift, scale, gate, w_in_ref, w_out_ref, ln_g, ln_b):
    h = (y * (1.0 + scale) + shift).astype(BF16)
    hid = w_out_ref.shape[0]
    acc = jnp.zeros(y.shape, F32)
    for c in range(hid // FFN_CHUNK):
        lo = c * FFN_CHUNK
        g_ = _dot(h, w_in_ref[:, lo:lo + FFN_CHUNK])
        u_ = _dot(h, w_in_ref[:, hid + lo:hid + lo + FFN_CHUNK])
        acc = acc + _dot((jax.nn.silu(g_) * u_).astype(BF16), w_out_ref[lo:lo + FFN_CHUNK, :])
    return _layer_norm(ALPHA * y + (1.0 + gate) * acc, ln_g, ln_b)


def _mix0_kernel(x_ref, on_ref, og_ref, gate1_ref, wo_ref, ln1g_ref, ln1b_ref,
                 shift2_ref, scale2_ref, gate2_ref, fin_ref, fout_ref, ln2g_ref, ln2b_ref, o_ref):
    half = on_ref.shape[2]
    out = _dot(on_ref[0], wo_ref[0:half, :]) + _dot(og_ref[0], wo_ref[half:2 * half, :])
    y = _layer_norm(ALPHA * x_ref[0] + (1.0 + gate1_ref[0]) * out, ln1g_ref[...], ln1b_ref[...])
    o_ref[0] = _swiglu_block(y, shift2_ref[0], scale2_ref[0], gate2_ref[0], fin_ref, fout_ref,
                             ln2g_ref[...], ln2b_ref[...])


def _mix0(x, o_nsa, o_gmlp, mod, w_out, ln_g, ln_b, ffn_w_in, ffn_w_out):
    b_, s_, d = x.shape
    tm = ROW_TILE
    wo = w_out.astype(BF16)
    fin = ffn_w_in.astype(BF16)
    fout = ffn_w_out.astype(BF16)
    row = lambda v: v.reshape(1, d)
    tok = lambda w: pl.BlockSpec((1, tm, w), lambda b, i: (b, i, 0))
    return pl.pallas_call(
        _mix0_kernel,
        grid=(b_, s_ // tm),
        in_specs=[tok(d), tok(o_nsa.shape[2]), tok(o_gmlp.shape[2]), _mod_spec(0, 2, b_, d),
                  _const_spec(wo.shape), _const_spec((1, d)), _const_spec((1, d)),
                  _mod_spec(1, 0, b_, d), _mod_spec(1, 1, b_, d), _mod_spec(1, 2, b_, d),
                  _const_spec(fin.shape), _const_spec(fout.shape), _const_spec((1, d)), _const_spec((1, d))],
        out_specs=tok(d),
        out_shape=jax.ShapeDtypeStruct(x.shape, F32),
        compiler_params=_params(2, VMEM_LIMIT),
        name="mix0",
    )(x, o_nsa, o_gmlp, mod, wo, row(ln_g[0]), row(ln_b[0]), mod, mod, mod, fin, fout, row(ln_g[1]), row(ln_b[1]))


def _layer1_kernel(x_ref, shift1_ref, scale1_ref, gate1_ref, win_ref, cw_ref, wo_ref, ln1g_ref, ln1b_ref,
                   shift2_ref, scale2_ref, gate2_ref, fin_ref, fout_ref, ln2g_ref, ln2b_ref, o_ref, carry_ref):
    i = pl.program_id(1)
    tm, d = x_ref.shape[1], x_ref.shape[2]
    halo = carry_ref.shape[0]

    @pl.when(i == 0)
    def _():
        carry_ref[...] = jnp.zeros(carry_ref.shape, F32)

    x = x_ref[0]
    h = (x * (1.0 + scale1_ref[0]) + shift1_ref[0]).astype(BF16)
    b_gate = _dot(h, win_ref[:, 0:d])
    cz = _dot(h, win_ref[:, d:2 * d]) * _dot(h, win_ref[:, 2 * d:3 * d])
    prev = carry_ref[...]
    carry_ref[...] = cz[tm - halo:tm]
    row = lax.broadcasted_iota(jnp.int32, (tm, 1), 0)
    back1 = jnp.where(row == 0, prev[halo - 1:halo], pltpu.roll(cz, 1, axis=0))
    back2 = jnp.where(row == 0, prev[halo - 2:halo - 1],
                      jnp.where(row == 1, prev[halo - 1:halo], pltpu.roll(cz, 2, axis=0)))
    cw = cw_ref[...]
    y = cw[0:1] * back2 + cw[1:2] * back1 + cw[2:3] * cz
    out = _dot((b_gate * y).astype(BF16), wo_ref[...])
    x1 = _layer_norm(ALPHA * x + (1.0 + gate1_ref[0]) * out, ln1g_ref[...], ln1b_ref[...])
    o_ref[0] = _swiglu_block(x1, shift2_ref[0], scale2_ref[0], gate2_ref[0], fin_ref, fout_ref,
                             ln2g_ref[...], ln2b_ref[...])


def _layer1(x, mod, w_in, conv_w, w_out, ln_g, ln_b, ffn_w_in, ffn_w_out):
    b_, s_, d = x.shape
    tm = ROW_TILE
    assert conv_w.shape[0] == CONV_WIDTH == 3
    win = w_in.astype(BF16)
    wo = w_out.astype(BF16)
    fin = ffn_w_in.astype(BF16)
    fout = ffn_w_out.astype(BF16)
    row = lambda v: v.reshape(1, d)
    tok = pl.BlockSpec((1, tm, d), lambda b, i: (b, i, 0))
    return pl.pallas_call(
        _layer1_kernel,
        grid=(b_, s_ // tm),
        in_specs=[tok, _mod_spec(2, 0, b_, d), _mod_spec(2, 1, b_, d), _mod_spec(2, 2, b_, d),
                  _const_spec(win.shape), _const_spec(conv_w.shape), _const_spec(wo.shape),
                  _const_spec((1, d)), _const_spec((1, d)),
                  _mod_spec(3, 0, b_, d), _mod_spec(3, 1, b_, d), _mod_spec(3, 2, b_, d),
                  _const_spec(fin.shape), _const_spec(fout.shape), _const_spec((1, d)), _const_spec((1, d))],
        out_specs=tok,
        out_shape=jax.ShapeDtypeStruct(x.shape, F32),
        scratch_shapes=[pltpu.VMEM((8, d), F32)],
        compiler_params=_params(2, VMEM_LIMIT),
        name="layer1",
    )(x, mod, mod, mod, win, conv_w, wo, row(ln_g[0]), row(ln_b[0]), mod, mod, mod, fin, fout,
      row(ln_g[1]), row(ln_b[1]))


def kernel(x, c, ada_w, ada_b, ln_g, ln_b, even_w_in, even_cmp_pos, even_cmp_w1, even_cmp_w2, even_gmlp_norm_g,
           even_gmlp_ws, even_gmlp_bs, even_w_out, odd_w_in, odd_conv_w, odd_w_out, ffn_w_in, ffn_w_out):
    assert ada_w.shape[0] == DEPTH == 2
    mod = _ada_modulation(c, ada_w, ada_b)
    q, kc, vc, ks, vs, kw, vw, gates, o_gmlp = _inproj0(
        x, mod, even_w_in[0], even_gmlp_norm_g[0], even_gmlp_ws[0], even_gmlp_bs[0])
    kcc, vcc = _compress(kc, vc, even_cmp_pos[0], even_cmp_w1[0], even_cmp_w2[0])
    o_nsa = _nsa(q, gates, kcc, vcc, ks, vs, kw, vw)
    x = _mix0(x, o_nsa, o_gmlp, mod, even_w_out[0], ln_g[0], ln_b[0], ffn_w_in[0], ffn_w_out[0])
    return _layer1(x, mod, odd_w_in[0], odd_conv_w[0], odd_w_out[0], ln_g[1], ln_b[1], ffn_w_in[1], ffn_w_out[1])
```

```python
import numpy as np
import jax
import jax.numpy as jnp
from jax import lax
from jax.experimental import pallas as pl
from jax.experimental.pallas import tpu as pltpu

HEAD_DIM = 64
NSA_Q_HEADS = 8
NSA_KV_HEADS = 2
NSA_GROUP = NSA_Q_HEADS // NSA_KV_HEADS
CMP_LEN = 32
CMP_STRIDE = 16
CMP_HIDDEN = 128
SEL_LEN = 64
SEL_TOP_N = 16
WINDOW = 512
N_BRANCHES = 3
GMLP_GROUPS = 8
GMLP_HEAD_DIM = 64
GMLP_CHUNK = 128
CONV_WIDTH = 3
DEPTH = 2
ALPHA = (2 * DEPTH) ** 0.25
LN_EPS = 1e-5
NEG_INF = -1e30
FORCE_SCORE = 1e4

LANES = 128
HALF = LANES // 2
HALF_SHIFT = 6
SEL_SHIFT = 6
Q_TILE = 128
KV_CHUNK = 512
V_ROWS = HEAD_DIM + 16
LOG2_E = 1.4426950408889634
ROW_TILE = 512
FFN_CHUNK = 256
VMEM_LIMIT = 56 * 1024 * 1024

F32 = jnp.float32
BF16 = jnp.bfloat16


def _dot(a, b):
    return jnp.dot(a, b, preferred_element_type=F32)


def _layer_norm(v, g, b):
    mu = jnp.mean(v, axis=-1, keepdims=True)
    d = v - mu
    var = jnp.mean(d * d, axis=-1, keepdims=True)
    return d * lax.rsqrt(var + LN_EPS) * g + b


def _const_spec(shape):
    zeros = (0,) * len(shape)
    return pl.BlockSpec(shape, lambda *_: zeros, pipeline_mode=pl.Buffered(1))


def _params(n_axes, vmem=None):
    return pltpu.CompilerParams(dimension_semantics=("arbitrary",) * n_axes, vmem_limit_bytes=vmem)


def _ada_kernel(c_ref, w_ref, b_ref, o_ref):
    a = jax.nn.silu(c_ref[...]).astype(BF16)
    o_ref[0] = _dot(a, w_ref[0].astype(BF16)) + b_ref[0]


def _ada_modulation(c, ada_w, ada_b):
    b_, d = c.shape
    n_pairs = ada_w.shape[0] * ada_w.shape[1]
    w = ada_w.reshape(n_pairs, d, 3 * d)
    bias = ada_b.reshape(n_pairs, 1, 3 * d)
    mod = pl.pallas_call(
        _ada_kernel,
        grid=(n_pairs, 3),
        in_specs=[
            pl.BlockSpec((b_, d), lambda i, j: (0, 0)),
            pl.BlockSpec((1, d, d), lambda i, j: (i, 0, j)),
            pl.BlockSpec((1, 1, d), lambda i, j: (i, 0, j)),
        ],
        out_specs=pl.BlockSpec((1, b_, d), lambda i, j: (i, 0, j)),
        out_shape=jax.ShapeDtypeStruct((n_pairs, b_, 3 * d), F32),
        compiler_params=_params(2),
        name="ada",
    )(c, w, bias)
    return mod.reshape(n_pairs * b_ * 3, 1, d)


def _mod_spec(pair, which, b_, d):
    return pl.BlockSpec((1, 1, d), lambda b, i: ((pair * b_ + b) * 3 + which, 0, 0))


def _segment_sum(x, seg):
    hi = x.astype(BF16)
    lo = (x - hi.astype(F32)).astype(BF16)
    return _dot(hi, seg) + _dot(lo, seg)


def _inproj0_kernel(x_ref, shift_ref, scale_ref, wq_ref, wkv_ref, wu_ref, wv_ref, seg_ref, ng_ref,
                    wsp_ref, bsp_ref,
                    q_ref, kc_ref, vc_ref, ks_ref, vs_ref, kw_ref, vw_ref, gate_ref, og_ref):
    tm = x_ref.shape[1]
    h = (x_ref[0] * (1.0 + scale_ref[0]) + shift_ref[0]).astype(BF16)
    q_ref[0] = _dot(h, wq_ref[...]) * (HEAD_DIM ** -0.5 * LOG2_E)
    kv = _dot(h, wkv_ref[...])
    for n, ref in enumerate((kc_ref, vc_ref, ks_ref, vs_ref, kw_ref, vw_ref)):
        ref[0] = kv[:, n * LANES:(n + 1) * LANES].astype(ref.dtype)
    gate_ref[0] = jax.nn.sigmoid(kv[:, 6 * LANES:7 * LANES])

    u = jax.nn.gelu(_dot(h, wu_ref[...]))
    vf = jax.nn.gelu(_dot(h, wv_ref[...]))
    seg = seg_ref[...]
    inv = 1.0 / GMLP_HEAD_DIM
    mu = _segment_sum(vf, seg) * inv
    dv = vf - mu
    var = _segment_sum(dv * dv, seg) * inv
    vn = dv * lax.rsqrt(var + LN_EPS) * ng_ref[...]

    left = lax.broadcasted_iota(jnp.int32, (GMLP_CHUNK, LANES), 1) < HALF
    t_idx = lax.broadcasted_iota(jnp.int32, (GMLP_CHUNK, 2 * GMLP_CHUNK), 0)
    s_idx = lax.broadcasted_iota(jnp.int32, (GMLP_CHUNK, 2 * GMLP_CHUNK), 1) & (GMLP_CHUNK - 1)
    causal = s_idx <= t_idx
    for p in range(GMLP_GROUPS // 2):
        cols = slice(p * LANES, (p + 1) * LANES)
        wcat = jnp.where(causal, wsp_ref[p], 0.0).astype(BF16)
        bias = bsp_ref[:, cols]
        for c in range(tm // GMLP_CHUNK):
            rows = slice(c * GMLP_CHUNK, (c + 1) * GMLP_CHUNK)
            vp = vn[rows, cols]
            v2 = jnp.concatenate([jnp.where(left, vp, 0.0), jnp.where(left, 0.0, vp)], axis=0).astype(BF16)
            mixed = _dot(wcat, v2)
            og_ref[0, rows, cols] = (u[rows, cols] * (mixed + bias)).astype(og_ref.dtype)


def _inproj0(x, mod, w_in, norm_g, w_s, b_s):
    b_, s_, d = x.shape
    tm = ROW_TILE
    qd = NSA_Q_HEADS * HEAD_DIM
    kvd = NSA_KV_HEADS * HEAD_DIM
    gd = NSA_Q_HEADS * N_BRANCHES
    gm = GMLP_GROUPS * GMLP_HEAD_DIM
    o_kv = qd
    o_g = qd + 6 * kvd
    o_u = o_g + gd
    o_v = o_u + gm
    wq = w_in[:, :qd].astype(BF16)
    wkv = jnp.concatenate([w_in[:, o_kv:o_g], w_in[:, o_g:o_u],
                           jnp.zeros((d, LANES - gd), w_in.dtype)], axis=1).astype(BF16)
    wu = w_in[:, o_u:o_v].astype(BF16)
    wv = w_in[:, o_v:o_v + gm].astype(BF16)
    seg = jnp.asarray(np.kron(np.eye(GMLP_GROUPS), np.ones((GMLP_HEAD_DIM, GMLP_HEAD_DIM))), BF16)
    ng = norm_g.reshape(1, gm)
    wsp = w_s.reshape(GMLP_GROUPS // 2, 2, GMLP_CHUNK, GMLP_CHUNK).transpose(0, 2, 1, 3).reshape(
        GMLP_GROUPS // 2, GMLP_CHUNK, 2 * GMLP_CHUNK)
    bsp = jnp.broadcast_to(b_s.T[:, :, None], (GMLP_CHUNK, GMLP_GROUPS, GMLP_HEAD_DIM)).reshape(GMLP_CHUNK, gm)

    tok = lambda w: pl.BlockSpec((1, tm, w), lambda b, i: (b, i, 0))
    shp = lambda w, dt: jax.ShapeDtypeStruct((b_, s_, w), dt)
    return pl.pallas_call(
        _inproj0_kernel,
        grid=(b_, s_ // tm),
        in_specs=[tok(d), _mod_spec(0, 0, b_, d), _mod_spec(0, 1, b_, d),
                  _const_spec(wq.shape), _const_spec(wkv.shape), _const_spec(wu.shape), _const_spec(wv.shape),
                  _const_spec(seg.shape), _const_spec(ng.shape), _const_spec(wsp.shape), _const_spec(bsp.shape)],
        out_specs=[tok(qd)] + [tok(LANES)] * 7 + [tok(gm)],
        out_shape=[shp(qd, F32), shp(LANES, F32), shp(LANES, F32)] + [shp(LANES, BF16)] * 4
                  + [shp(LANES, F32), shp(gm, BF16)],
        compiler_params=_params(2, VMEM_LIMIT),
        name="inproj0",
    )(x, mod, mod, wq, wkv, wu, wv, seg, ng, wsp, bsp)


def _compress_kernel(kch_ref, vch_ref, pos_ref, w1a_ref, w1b_ref, w2_ref, kco_ref, vco_ref):
    for t, (src, dst) in enumerate(((kch_ref, kco_ref), (vch_ref, vco_ref))):
        ch = src[0]
        n = ch.shape[0]
        first = (ch + pos_ref[t, 0:1]).astype(BF16)
        second = (ch + pos_ref[t, 1:2]).astype(BF16)
        ha = _dot(first, w1a_ref[t])
        hb = _dot(second, w1b_ref[t])
        h1 = ha + pltpu.roll(hb, n - 1, axis=0)
        dst[0] = _dot(jax.nn.gelu(h1).astype(BF16), w2_ref[t]).astype(dst.dtype)


def _compress(kc, vc, cmp_pos, cmp_w1, cmp_w2):
    b_, s_, _ = kc.shape
    nch = s_ // CMP_STRIDE
    r = CMP_LEN // CMP_STRIDE
    assert r == 2
    eye = jnp.eye(NSA_KV_HEADS, dtype=cmp_w1.dtype)
    w1 = cmp_w1.reshape(2, r, CMP_STRIDE, HEAD_DIM, CMP_HIDDEN)
    w1 = jnp.einsum("thldc,gk->thlgdkc", w1, eye).reshape(
        2, r, CMP_STRIDE * LANES, NSA_KV_HEADS * CMP_HIDDEN).astype(BF16)
    w2 = jnp.einsum("tcd,gk->tgckd", cmp_w2, eye).reshape(2, NSA_KV_HEADS * CMP_HIDDEN, LANES).astype(BF16)
    pos = jnp.broadcast_to(cmp_pos.reshape(2, r, CMP_STRIDE, 1, HEAD_DIM),
                           (2, r, CMP_STRIDE, NSA_KV_HEADS, HEAD_DIM)).reshape(2, r, CMP_STRIDE * LANES)
    kch = kc.reshape(b_, nch, CMP_STRIDE * LANES)
    vch = vc.reshape(b_, nch, CMP_STRIDE * LANES)
    ch_spec = pl.BlockSpec((1, nch, CMP_STRIDE * LANES), lambda b: (b, 0, 0))
    out_spec = pl.BlockSpec((1, nch, LANES), lambda b: (b, 0, 0))
    out_shape = jax.ShapeDtypeStruct((b_, nch, LANES), BF16)
    return pl.pallas_call(
        _compress_kernel,
        grid=(b_,),
        in_specs=[ch_spec, ch_spec, _const_spec(pos.shape), _const_spec(w1[:, 0].shape),
                  _const_spec(w1[:, 1].shape), _const_spec(w2.shape)],
        out_specs=[out_spec, out_spec],
        out_shape=[out_shape, out_shape],
        compiler_params=_params(1, VMEM_LIMIT),
        name="compress",
    )(kch, vch, pos, w1[:, 0], w1[:, 1], w2)


def _masked_exp_t(s, mask):
    sm = jnp.where(mask, s, NEG_INF)
    return jnp.where(mask, jnp.exp2(sm - jnp.max(sm, axis=0, keepdims=True)), 0.0)


def _safe_reciprocal(l):
    return jnp.where(l > 0.0, 1.0 / l, 0.0)


def _nsa_kernel(q_ref, gate_ref, kc_ref, vc_ref, ks_ref, vs_ref, kw_ref, vw_ref, cover_ref, o_ref,
                kaug_ref, vst_ref, vwt_ref, vct_ref, s0_ref, s1_ref):
    i = pl.program_id(1)
    start = i * Q_TILE
    s_len = ks_ref.shape[1]
    n_sel = s_len // SEL_LEN
    n_cmp = kc_ref.shape[1]
    rows = NSA_GROUP * Q_TILE
    sub = KV_CHUNK // LANES

    @pl.when(i == 0)
    def _():
        def build(c, carry):
            off = pl.multiple_of(c * KV_CHUNK, KV_CHUNK)
            key_blk = (off + lax.broadcasted_iota(jnp.int32, (KV_CHUNK, LANES), 0)) >> SEL_SHIFT
            ln = lax.broadcasted_iota(jnp.int32, (KV_CHUNK, LANES), 1)
            ks = ks_ref[0, pl.ds(off, KV_CHUNK), :].astype(F32)
            for g in range(NSA_KV_HEADS):
                onehot = (ln - (1 - g) * HALF == key_blk).astype(F32)
                kaug_ref[g, pl.ds(off, KV_CHUNK), :] = jnp.where((ln >> HALF_SHIFT) == g, ks, onehot).astype(BF16)
            for j in range(sub):
                blk = pl.ds(pl.multiple_of(off + j * LANES, LANES), LANES)
                cols = slice(j * LANES, (j + 1) * LANES)
                vs_t = vs_ref[0, blk, :].astype(F32).T.astype(BF16)
                vw_t = vw_ref[0, blk, :].astype(F32).T.astype(BF16)
                for g in range(NSA_KV_HEADS):
                    vst_ref[c, g, 0:HEAD_DIM, cols] = vs_t[g * HALF:(g + 1) * HALF]
                    vwt_ref[c * sub + j, g, 0:HEAD_DIM, :] = vw_t[g * HALF:(g + 1) * HALF]
            return carry
        lax.fori_loop(0, s_len // KV_CHUNK, build, 0)
        for g in range(NSA_KV_HEADS):
            for j in range(n_cmp // LANES):
                cols = slice(j * LANES, (j + 1) * LANES)
                vct_ref[g, 0:HEAD_DIM, cols] = vc_ref[0, cols, :].astype(F32).T[g * HALF:(g + 1) * HALF].astype(BF16)
        ones = lambda ref: jnp.ones(ref.shape[:-2] + (V_ROWS - HEAD_DIM, ref.shape[-1]), BF16)
        vst_ref[:, :, HEAD_DIM:V_ROWS, :] = ones(vst_ref)
        vwt_ref[:, :, HEAD_DIM:V_ROWS, :] = ones(vwt_ref)
        vct_ref[:, HEAD_DIM:V_ROWS, :] = ones(vct_ref)

    q = q_ref[0]
    q_t = jnp.concatenate([q[:, c * LANES:(c + 1) * LANES].T for c in range(q.shape[1] // LANES)], axis=0)
    g_t = gate_ref[0].T
    qpos = start + (lax.broadcasted_iota(jnp.int32, (1, rows), 1) & (Q_TILE - 1))
    tpos = start + lax.broadcasted_iota(jnp.int32, (1, Q_TILE), 1)
    j_idx = lax.broadcasted_iota(jnp.int32, (n_sel, 1), 0)
    zeros_half = jnp.zeros((HALF, rows), F32)
    halves = lambda g, own, other: jnp.concatenate([own, other] if g == 0 else [other, own], axis=0).astype(BF16)

    q0, qa, o_cmp = [], [], []
    for g in range(NSA_KV_HEADS):
        h0 = NSA_GROUP * g
        qg = jnp.concatenate([q_t[(h0 + r) * HEAD_DIM:(h0 + r + 1) * HEAD_DIM, :] for r in range(NSA_GROUP)], axis=1)
        q0.append(halves(g, qg, zeros_half))

        s = _dot(kc_ref[0], q0[g])
        cmp_end = lax.broadcasted_iota(jnp.int32, (n_cmp, 1), 0) * CMP_STRIDE + (CMP_LEN - 1)
        e = _masked_exp_t(s, cmp_end <= qpos)
        nd = _dot(vct_ref[g], e.astype(BF16))
        inv = _safe_reciprocal(nd[HEAD_DIM:HEAD_DIM + 1])
        o_cmp.append(nd[0:HEAD_DIM] * inv)
        p = e * inv
        p_sum = p[:, 0:Q_TILE]
        for r in range(1, NSA_GROUP):
            p_sum = p_sum + p[:, r * Q_TILE:(r + 1) * Q_TILE]
        imp = jnp.dot(cover_ref[...], p_sum, precision=lax.Precision.HIGHEST,
                      preferred_element_type=F32)

        cur = tpos >> SEL_SHIFT
        forced = (j_idx == 0) | (j_idx == cur) | (j_idx == cur - 1)
        score = jnp.where(forced, FORCE_SCORE, jnp.where(j_idx * SEL_LEN <= tpos, imp, -FORCE_SCORE))
        rank = jnp.zeros((n_sel, Q_TILE), jnp.int32)
        for jp in range(n_sel):
            other = score[jp:jp + 1, :]
            beats = (other > score) | ((other == score) & (jp < j_idx))
            rank = rank + beats.astype(jnp.int32)
        bias = jnp.where(rank < min(SEL_TOP_N, n_sel), 0.0, NEG_INF)
        if n_sel < HALF:
            bias = jnp.concatenate([bias, jnp.zeros((HALF - n_sel, Q_TILE), F32)], axis=0)
        qa.append(halves(g, qg, jnp.concatenate([bias] * NSA_GROUP, axis=1)))

    diag = lax.div(i, KV_CHUNK // Q_TILE)

    def scores(g, c):
        off = pl.multiple_of(c * KV_CHUNK, KV_CHUNK)
        return _dot(kaug_ref[g, pl.ds(off, KV_CHUNK), :], qa[g])

    sbuf = (s0_ref, s1_ref)
    strip_max = lambda s: jnp.max(s.reshape(KV_CHUNK // 8, 8, rows), axis=0)
    causal = diag * KV_CHUNK + lax.broadcasted_iota(jnp.int32, (KV_CHUNK, 1), 0) <= qpos

    def qk_stage(g, c, masked=False):
        s = scores(g, c)
        if masked:
            s = jnp.where(causal, s, NEG_INF)
        sbuf[g][...] = s
        return strip_max(s)

    def pv_stage(g, c, m8, state):
        m, acc = state
        m_new = jnp.maximum(m, jnp.max(m8, axis=0, keepdims=True))
        p = jnp.exp2(sbuf[g][...] - m_new).astype(BF16)
        return m_new, jnp.exp2(m - m_new) * acc + _dot(vst_ref[c, g], p)

    def sel_chunk(c, carry):
        m8_0, st0, st1 = carry
        m8_1 = qk_stage(1, c)
        st0 = pv_stage(0, c, m8_0, st0)
        m8_0 = qk_stage(0, c + 1)
        st1 = pv_stage(1, c, m8_1, st1)
        return m8_0, st0, st1

    init = (jnp.full((1, rows), NEG_INF, F32), jnp.zeros((V_ROWS, rows), F32))
    _, st0, st1 = lax.fori_loop(0, diag, sel_chunk, (qk_stage(0, 0), init, init))
    s_d0 = jnp.where(causal, s0_ref[...], NEG_INF)
    s0_ref[...] = s_d0
    m8_1 = qk_stage(1, diag, masked=True)
    acc_sel = (pv_stage(0, diag, strip_max(s_d0), st0)[1], pv_stage(1, diag, m8_1, st1)[1])

    n_blk = (WINDOW + Q_TILE) // LANES
    wb = jnp.maximum(i - WINDOW // Q_TILE, 0)
    ws = pl.multiple_of(wb * LANES, LANES)
    heads_out = []
    for g in range(NSA_KV_HEADS):
        acc = acc_sel[g]
        o_sel = acc[0:HEAD_DIM] * (1.0 / acc[HEAD_DIM:HEAD_DIM + 1])
        sw = _dot(kw_ref[0, pl.ds(ws, n_blk * LANES), :], q0[g])
        dist = qpos - (ws + lax.broadcasted_iota(jnp.int32, (n_blk * LANES, 1), 0))
        ew = _masked_exp_t(sw, (dist >= 0) & (dist < WINDOW)).astype(BF16)
        nd = _dot(vwt_ref[wb, g], ew[0:LANES])
        for j in range(1, n_blk):
            nd = nd + _dot(vwt_ref[wb + j, g], ew[j * LANES:(j + 1) * LANES])
        o_win = nd[0:HEAD_DIM] * (1.0 / nd[HEAD_DIM:HEAD_DIM + 1])

        for r in range(NSA_GROUP):
            sl = slice(r * Q_TILE, (r + 1) * Q_TILE)
            c0 = (NSA_GROUP * g + r) * N_BRANCHES
            heads_out.append(g_t[c0:c0 + 1] * o_cmp[g][:, sl] + g_t[c0 + 1:c0 + 2] * o_sel[:, sl]
                             + g_t[c0 + 2:c0 + 3] * o_win[:, sl])
    for c in range(len(heads_out) // 2):
        pair = jnp.concatenate(heads_out[2 * c:2 * c + 2], axis=0)
        o_ref[0, :, c * LANES:(c + 1) * LANES] = pair.T.astype(o_ref.dtype)


def _cover_matrix_t(n_sel, n_cmp_padded):
    jj = np.arange(n_sel)[:, None]
    kk = np.arange(n_cmp_padded)[None, :]
    cover = ((kk * CMP_STRIDE < (jj + 1) * SEL_LEN) & (kk * CMP_STRIDE + CMP_LEN > jj * SEL_LEN)
             & (kk < n_cmp_padded - 1))
    return jnp.asarray(cover, F32)


def _nsa(q, gates, kcc, vcc, ks, vs, kw, vw):
    b_, s_, qd = q.shape
    n_cmp = kcc.shape[1]
    n_sel = s_ // SEL_LEN
    assert Q_TILE == LANES and n_sel <= HALF and n_sel % 8 == 0 and n_cmp % LANES == 0
    assert s_ % KV_CHUNK == 0 and s_ >= WINDOW + Q_TILE and WINDOW % Q_TILE == 0
    cover = _cover_matrix_t(n_sel, n_cmp)
    tok = lambda w: pl.BlockSpec((1, Q_TILE, w), lambda b, i: (b, i, 0))
    seq = lambda n: pl.BlockSpec((1, n, LANES), lambda b, i: (b, 0, 0))
    return pl.pallas_call(
        _nsa_kernel,
        grid=(b_, s_ // Q_TILE),
        in_specs=[tok(qd), tok(LANES), seq(n_cmp), seq(n_cmp), seq(s_), seq(s_), seq(s_), seq(s_),
                  _const_spec(cover.shape)],
        out_specs=tok(qd),
        out_shape=jax.ShapeDtypeStruct((b_, s_, qd), BF16),
        scratch_shapes=[pltpu.VMEM((NSA_KV_HEADS, s_, LANES), BF16),
                        pltpu.VMEM((s_ // KV_CHUNK, NSA_KV_HEADS, V_ROWS, KV_CHUNK), BF16),
                        pltpu.VMEM((s_ // LANES, NSA_KV_HEADS, V_ROWS, LANES), BF16),
                        pltpu.VMEM((NSA_KV_HEADS, V_ROWS, n_cmp), BF16),
                        pltpu.VMEM((KV_CHUNK, NSA_GROUP * Q_TILE), F32),
                        pltpu.VMEM((KV_CHUNK, NSA_GROUP * Q_TILE), F32)],
        compiler_params=_params(2, VMEM_LIMIT),
        name="nsa",
    )(q, gates, kcc, vcc, ks, vs, kw, vw, cover)


def _swiglu_block(y, shift, scale, gate, w_in_ref, w_out_ref, ln_g, ln_b):
    h = (y * (1.0 + scale) + shift).astype(BF16)
    hid = w_out_ref.shape[0]
    acc = jnp.zeros(y.shape, F32)
    for c in range(hid // FFN_CHUNK):
        lo = c * FFN_CHUNK
        g_ = _dot(h, w_in_ref[:, lo:lo + FFN_CHUNK])
        u_ = _dot(h, w_in_ref[:, hid + lo:hid + lo + FFN_CHUNK])
        acc = acc + _dot((jax.nn.silu(g_) * u_).astype(BF16), w_out_ref[lo:lo + FFN_CHUNK, :])
    return _layer_norm(ALPHA * y + (1.0 + gate) * acc, ln_g, ln_b)


def _mix0_kernel(x_ref, on_ref, og_ref, gate1_ref, wo_ref, ln1g_ref, ln1b_ref,
                 shift2_ref, scale2_ref, gate2_ref, fin_ref, fout_ref, ln2g_ref, ln2b_ref, o_ref):
    half = on_ref.shape[2]
    out = _dot(on_ref[0], wo_ref[0:half, :]) + _dot(og_ref[0], wo_ref[half:2 * half, :])
    y = _layer_norm(ALPHA * x_ref[0] + (1.0 + gate1_ref[0]) * out, ln1g_ref[...], ln1b_ref[...])
    o_ref[0] = _swiglu_block(y, shift2_ref[0], scale2_ref[0], gate2_ref[0], fin_ref, fout_ref,
                             ln2g_ref[...], ln2b_ref[...])


def _mix0(x, o_nsa, o_gmlp, mod, w_out, ln_g, ln_b, ffn_w_in, ffn_w_out):
    b_, s_, d = x.shape
    tm = ROW_TILE
    wo = w_out.astype(BF16)
    fin = ffn_w_in.astype(BF16)
    fout = ffn_w_out.astype(BF16)
    row = lambda v: v.reshape(1, d)
    tok = lambda w: pl.BlockSpec((1, tm, w), lambda b, i: (b, i, 0))
    return pl.pallas_call(
        _mix0_kernel,
        grid=(b_, s_ // tm),
        in_specs=[tok(d), tok(o_nsa.shape[2]), tok(o_gmlp.shape[2]), _mod_spec(0, 2, b_, d),
                  _const_spec(wo.shape), _const_spec((1, d)), _const_spec((1, d)),
                  _mod_spec(1, 0, b_, d), _mod_spec(1, 1, b_, d), _mod_spec(1, 2, b_, d),
                  _const_spec(fin.shape), _const_spec(fout.shape), _const_spec((1, d)), _const_spec((1, d))],
        out_specs=tok(d),
        out_shape=jax.ShapeDtypeStruct(x.shape, F32),
        compiler_params=_params(2, VMEM_LIMIT),
        name="mix0",
    )(x, o_nsa, o_gmlp, mod, wo, row(ln_g[0]), row(ln_b[0]), mod, mod, mod, fin, fout, row(ln_g[1]), row(ln_b[1]))


def _layer1_kernel(x_ref, shift1_ref, scale1_ref, gate1_ref, win_ref, cw_ref, wo_ref, ln1g_ref, ln1b_ref,
                   shift2_ref, scale2_ref, gate2_ref, fin_ref, fout_ref, ln2g_ref, ln2b_ref, o_ref, carry_ref):
    i = pl.program_id(1)
    tm, d = x_ref.shape[1], x_ref.shape[2]
    halo = carry_ref.shape[0]

    @pl.when(i == 0)
    def _():
        carry_ref[...] = jnp.zeros(carry_ref.shape, F32)

    x = x_ref[0]
    h = (x * (1.0 + scale1_ref[0]) + shift1_ref[0]).astype(BF16)
    b_gate = _dot(h, win_ref[:, 0:d])
    cz = _dot(h, win_ref[:, d:2 * d]) * _dot(h, win_ref[:, 2 * d:3 * d])
    prev = carry_ref[...]
    carry_ref[...] = cz[tm - halo:tm]
    row = lax.broadcasted_iota(jnp.int32, (tm, 1), 0)
    back1 = jnp.where(row == 0, prev[halo - 1:halo], pltpu.roll(cz, 1, axis=0))
    back2 = jnp.where(row == 0, prev[halo - 2:halo - 1],
                      jnp.where(row == 1, prev[halo - 1:halo], pltpu.roll(cz, 2, axis=0)))
    cw = cw_ref[...]
    y = cw[0:1] * back2 + cw[1:2] * back1 + cw[2:3] * cz
    out = _dot((b_gate * y).astype(BF16), wo_ref[...])
    x1 = _layer_norm(ALPHA * x + (1.0 + gate1_ref[0]) * out, ln1g_ref[...], ln1b_ref[...])
    o_ref[0] = _swiglu_block(x1, shift2_ref[0], scale2_ref[0], gate2_ref[0], fin_ref, fout_ref,
                             ln2g_ref[...], ln2b_ref[...])


def _layer1(x, mod, w_in, conv_w, w_out, ln_g, ln_b, ffn_w_in, ffn_w_out):
    b_, s_, d = x.shape
    tm = ROW_TILE
    assert conv_w.shape[0] == CONV_WIDTH == 3
    win = w_in.astype(BF16)
    wo = w_out.astype(BF16)
    fin = ffn_w_in.astype(BF16)
    fout = ffn_w_out.astype(BF16)
    row = lambda v: v.reshape(1, d)
    tok = pl.BlockSpec((1, tm, d), lambda b, i: (b, i, 0))
    return pl.pallas_call(
        _layer1_kernel,
        grid=(b_, s_ // tm),
        in_specs=[tok, _mod_spec(2, 0, b_, d), _mod_spec(2, 1, b_, d), _mod_spec(2, 2, b_, d),
                  _const_spec(win.shape), _const_spec(conv_w.shape), _const_spec(wo.shape),
                  _const_spec((1, d)), _const_spec((1, d)),
                  _mod_spec(3, 0, b_, d), _mod_spec(3, 1, b_, d), _mod_spec(3, 2, b_, d),
                  _const_spec(fin.shape), _const_spec(fout.shape), _const_spec((1, d)), _const_spec((1, d))],
        out_specs=tok,
        out_shape=jax.ShapeDtypeStruct(x.shape, F32),
        scratch_shapes=[pltpu.VMEM((8, d), F32)],
        compiler_params=_params(2, VMEM_LIMIT),
        name="layer1",
    )(x, mod, mod, mod, win, conv_w, wo, row(ln_g[0]), row(ln_b[0]), mod, mod, mod, fin, fout,
      row(ln_g[1]), row(ln_b[1]))


def kernel(x, c, ada_w, ada_b, ln_g, ln_b, even_w_in, even_cmp_pos, even_cmp_w1, even_cmp_w2, even_gmlp_norm_g,
           even_gmlp_ws, even_gmlp_bs, even_w_out, odd_w_in, odd_conv_w, odd_w_out, ffn_w_in, ffn_w_out):
    assert ada_w.shape[0] == DEPTH == 2
    mod = _ada_modulation(c, ada_w, ada_b)
    q, kc, vc, ks, vs, kw, vw, gates, o_gmlp = _inproj0(
        x, mod, even_w_in[0], even_gmlp_norm_g[0], even_gmlp_ws[0], even_gmlp_bs[0])
    kcc, vcc = _compress(kc, vc, even_cmp_pos[0], even_cmp_w1[0], even_cmp_w2[0])
    o_nsa = _nsa(q, gates, kcc, vcc, ks, vs, kw, vw)
    x = _mix0(x, o_nsa, o_gmlp, mod, even_w_out[0], ln_g[0], ln_b[0], ffn_w_in[0], ffn_w_out[0])
    return _layer1(x, mod, odd_w_in[0], odd_conv_w[0], odd_w_out[0], ln_g[1], ln_b[1], ffn_w_in[1], ffn_w_out[1])
```

```python
import numpy as np
import jax
import jax.numpy as jnp
from jax import lax
from jax.experimental import pallas as pl
from jax.experimental.pallas import tpu as pltpu

HEAD_DIM = 64
NSA_Q_HEADS = 8
NSA_KV_HEADS = 2
NSA_GROUP = NSA_Q_HEADS // NSA_KV_HEADS
CMP_LEN = 32
CMP_STRIDE = 16
CMP_HIDDEN = 128
SEL_LEN = 64
SEL_TOP_N = 16
WINDOW = 512
N_BRANCHES = 3
GMLP_GROUPS = 8
GMLP_HEAD_DIM = 64
GMLP_CHUNK = 128
CONV_WIDTH = 3
DEPTH = 2
ALPHA = (2 * DEPTH) ** 0.25
LN_EPS = 1e-5
NEG_INF = -1e30
FORCE_SCORE = 1e4

LANES = 128
HALF = LANES // 2
HALF_SHIFT = 6
SEL_SHIFT = 6
Q_TILE = 256
KV_CHUNK = 512
V_ROWS = HEAD_DIM + 16
LOG2_E = 1.4426950408889634
ROW_TILE = 512
FFN_CHUNK = 256
VMEM_LIMIT = 56 * 1024 * 1024

F32 = jnp.float32
BF16 = jnp.bfloat16


def _dot(a, b):
    return jnp.dot(a, b, preferred_element_type=F32)


def _layer_norm(v, g, b):
    mu = jnp.mean(v, axis=-1, keepdims=True)
    d = v - mu
    var = jnp.mean(d * d, axis=-1, keepdims=True)
    return d * lax.rsqrt(var + LN_EPS) * g + b


def _const_spec(shape):
    zeros = (0,) * len(shape)
    return pl.BlockSpec(shape, lambda *_: zeros, pipeline_mode=pl.Buffered(1))


def _params(n_axes, vmem=None):
    return pltpu.CompilerParams(dimension_semantics=("arbitrary",) * n_axes, vmem_limit_bytes=vmem)


def _ada_kernel(c_ref, w_ref, b_ref, o_ref):
    a = jax.nn.silu(c_ref[...]).astype(BF16)
    o_ref[0] = _dot(a, w_ref[0].astype(BF16)) + b_ref[0]


def _ada_modulation(c, ada_w, ada_b):
    b_, d = c.shape
    n_pairs = ada_w.shape[0] * ada_w.shape[1]
    w = ada_w.reshape(n_pairs, d, 3 * d)
    bias = ada_b.reshape(n_pairs, 1, 3 * d)
    mod = pl.pallas_call(
        _ada_kernel,
        grid=(n_pairs, 3),
        in_specs=[
            pl.BlockSpec((b_, d), lambda i, j: (0, 0)),
            pl.BlockSpec((1, d, d), lambda i, j: (i, 0, j)),
            pl.BlockSpec((1, 1, d), lambda i, j: (i, 0, j)),
        ],
        out_specs=pl.BlockSpec((1, b_, d), lambda i, j: (i, 0, j)),
        out_shape=jax.ShapeDtypeStruct((n_pairs, b_, 3 * d), F32),
        compiler_params=_params(2),
        name="ada",
    )(c, w, bias)
    return mod.reshape(n_pairs * b_ * 3, 1, d)


def _mod_spec(pair, which, b_, d):
    return pl.BlockSpec((1, 1, d), lambda b, i: ((pair * b_ + b) * 3 + which, 0, 0))


def _segment_sum(x, seg):
    hi = x.astype(BF16)
    lo = (x - hi.astype(F32)).astype(BF16)
    return _dot(hi, seg) + _dot(lo, seg)


def _inproj0_kernel(x_ref, shift_ref, scale_ref, wq_ref, wkv_ref, wu_ref, wv_ref, seg_ref, ng_ref,
                    wsp_ref, bsp_ref,
                    q_ref, kc_ref, vc_ref, ks_ref, vs_ref, kw_ref, vw_ref, gate_ref, og_ref):
    tm = x_ref.shape[1]
    h = (x_ref[0] * (1.0 + scale_ref[0]) + shift_ref[0]).astype(BF16)
    q_ref[0] = _dot(h, wq_ref[...]) * (HEAD_DIM ** -0.5 * LOG2_E)
    kv = _dot(h, wkv_ref[...])
    for n, ref in enumerate((kc_ref, vc_ref, ks_ref, vs_ref, kw_ref, vw_ref)):
        ref[0] = kv[:, n * LANES:(n + 1) * LANES].astype(ref.dtype)
    gate_ref[0] = jax.nn.sigmoid(kv[:, 6 * LANES:7 * LANES])

    u = jax.nn.gelu(_dot(h, wu_ref[...]))
    vf = jax.nn.gelu(_dot(h, wv_ref[...]))
    seg = seg_ref[...]
    inv = 1.0 / GMLP_HEAD_DIM
    mu = _segment_sum(vf, seg) * inv
    dv = vf - mu
    var = _segment_sum(dv * dv, seg) * inv
    vn = dv * lax.rsqrt(var + LN_EPS) * ng_ref[...]

    left = lax.broadcasted_iota(jnp.int32, (GMLP_CHUNK, LANES), 1) < HALF
    t_idx = lax.broadcasted_iota(jnp.int32, (GMLP_CHUNK, 2 * GMLP_CHUNK), 0)
    s_idx = lax.broadcasted_iota(jnp.int32, (GMLP_CHUNK, 2 * GMLP_CHUNK), 1) & (GMLP_CHUNK - 1)
    causal = s_idx <= t_idx
    for p in range(GMLP_GROUPS // 2):
        cols = slice(p * LANES, (p + 1) * LANES)
        wcat = jnp.where(causal, wsp_ref[p], 0.0).astype(BF16)
        bias = bsp_ref[:, cols]
        for c in range(tm // GMLP_CHUNK):
            rows = slice(c * GMLP_CHUNK, (c + 1) * GMLP_CHUNK)
            vp = vn[rows, cols]
            v2 = jnp.concatenate([jnp.where(left, vp, 0.0), jnp.where(left, 0.0, vp)], axis=0).astype(BF16)
            mixed = _dot(wcat, v2)
            og_ref[0, rows, cols] = (u[rows, cols] * (mixed + bias)).astype(og_ref.dtype)


def _inproj0(x, mod, w_in, norm_g, w_s, b_s):
    b_, s_, d = x.shape
    tm = ROW_TILE
    qd = NSA_Q_HEADS * HEAD_DIM
    kvd = NSA_KV_HEADS * HEAD_DIM
    gd = NSA_Q_HEADS * N_BRANCHES
    gm = GMLP_GROUPS * GMLP_HEAD_DIM
    o_kv = qd
    o_g = qd + 6 * kvd
    o_u = o_g + gd
    o_v = o_u + gm
    wq = w_in[:, :qd].astype(BF16)
    wkv = jnp.concatenate([w_in[:, o_kv:o_g], w_in[:, o_g:o_u],
                           jnp.zeros((d, LANES - gd), w_in.dtype)], axis=1).astype(BF16)
    wu = w_in[:, o_u:o_v].astype(BF16)
    wv = w_in[:, o_v:o_v + gm].astype(BF16)
    seg = jnp.asarray(np.kron(np.eye(GMLP_GROUPS), np.ones((GMLP_HEAD_DIM, GMLP_HEAD_DIM))), BF16)
    ng = norm_g.reshape(1, gm)
    wsp = w_s.reshape(GMLP_GROUPS // 2, 2, GMLP_CHUNK, GMLP_CHUNK).transpose(0, 2, 1, 3).reshape(
        GMLP_GROUPS // 2, GMLP_CHUNK, 2 * GMLP_CHUNK)
    bsp = jnp.broadcast_to(b_s.T[:, :, None], (GMLP_CHUNK, GMLP_GROUPS, GMLP_HEAD_DIM)).reshape(GMLP_CHUNK, gm)

    tok = lambda w: pl.BlockSpec((1, tm, w), lambda b, i: (b, i, 0))
    shp = lambda w, dt: jax.ShapeDtypeStruct((b_, s_, w), dt)
    return pl.pallas_call(
        _inproj0_kernel,
        grid=(b_, s_ // tm),
        in_specs=[tok(d), _mod_spec(0, 0, b_, d), _mod_spec(0, 1, b_, d),
                  _const_spec(wq.shape), _const_spec(wkv.shape), _const_spec(wu.shape), _const_spec(wv.shape),
                  _const_spec(seg.shape), _const_spec(ng.shape), _const_spec(wsp.shape), _const_spec(bsp.shape)],
        out_specs=[tok(qd)] + [tok(LANES)] * 7 + [tok(gm)],
        out_shape=[shp(qd, F32), shp(LANES, F32), shp(LANES, F32)] + [shp(LANES, BF16)] * 4
                  + [shp(LANES, F32), shp(gm, BF16)],
        compiler_params=_params(2, VMEM_LIMIT),
        name="inproj0",
    )(x, mod, mod, wq, wkv, wu, wv, seg, ng, wsp, bsp)


def _compress_kernel(kch_ref, vch_ref, pos_ref, w1a_ref, w1b_ref, w2_ref, kco_ref, vco_ref):
    for t, (src, dst) in enumerate(((kch_ref, kco_ref), (vch_ref, vco_ref))):
        ch = src[0]
        n = ch.shape[0]
        first = (ch + pos_ref[t, 0:1]).astype(BF16)
        second = (ch + pos_ref[t, 1:2]).astype(BF16)
        ha = _dot(first, w1a_ref[t])
        hb = _dot(second, w1b_ref[t])
        h1 = ha + pltpu.roll(hb, n - 1, axis=0)
        dst[0] = _dot(jax.nn.gelu(h1).astype(BF16), w2_ref[t]).astype(dst.dtype)


def _compress(kc, vc, cmp_pos, cmp_w1, cmp_w2):
    b_, s_, _ = kc.shape
    nch = s_ // CMP_STRIDE
    r = CMP_LEN // CMP_STRIDE
    assert r == 2
    eye = jnp.eye(NSA_KV_HEADS, dtype=cmp_w1.dtype)
    w1 = cmp_w1.reshape(2, r, CMP_STRIDE, HEAD_DIM, CMP_HIDDEN)
    w1 = jnp.einsum("thldc,gk->thlgdkc", w1, eye).reshape(
        2, r, CMP_STRIDE * LANES, NSA_KV_HEADS * CMP_HIDDEN).astype(BF16)
    w2 = jnp.einsum("tcd,gk->tgckd", cmp_w2, eye).reshape(2, NSA_KV_HEADS * CMP_HIDDEN, LANES).astype(BF16)
    pos = jnp.broadcast_to(cmp_pos.reshape(2, r, CMP_STRIDE, 1, HEAD_DIM),
                           (2, r, CMP_STRIDE, NSA_KV_HEADS, HEAD_DIM)).reshape(2, r, CMP_STRIDE * LANES)
    kch = kc.reshape(b_, nch, CMP_STRIDE * LANES)
    vch = vc.reshape(b_, nch, CMP_STRIDE * LANES)
    ch_spec = pl.BlockSpec((1, nch, CMP_STRIDE * LANES), lambda b: (b, 0, 0))
    out_spec = pl.BlockSpec((1, nch, LANES), lambda b: (b, 0, 0))
    out_shape = jax.ShapeDtypeStruct((b_, nch, LANES), BF16)
    return pl.pallas_call(
        _compress_kernel,
        grid=(b_,),
        in_specs=[ch_spec, ch_spec, _const_spec(pos.shape), _const_spec(w1[:, 0].shape),
                  _const_spec(w1[:, 1].shape), _const_spec(w2.shape)],
        out_specs=[out_spec, out_spec],
        out_shape=[out_shape, out_shape],
        compiler_params=_params(1, VMEM_LIMIT),
        name="compress",
    )(kch, vch, pos, w1[:, 0], w1[:, 1], w2)


def _masked_exp_t(s, bias):
    sm = s + bias
    return jnp.exp2(sm - jnp.max(sm, axis=0, keepdims=True))


def _safe_reciprocal(l):
    return jnp.where(l > 0.0, 1.0 / l, 0.0)


def _nsa_kernel(q_ref, gate_ref, kc_ref, vc_ref, ks_ref, vs_ref, kw_ref, vw_ref, cover_ref, o_ref,
                kaug_ref, vst_ref, vwt_ref, vct_ref, s0_ref, s1_ref):
    i = pl.program_id(1)
    start = i * Q_TILE
    s_len = ks_ref.shape[1]
    n_sel = s_len // SEL_LEN
    n_cmp = kc_ref.shape[1]
    rows = NSA_GROUP * Q_TILE
    sub = KV_CHUNK // LANES

    @pl.when(i == 0)
    def _():
        def build(c, carry):
            off = pl.multiple_of(c * KV_CHUNK, KV_CHUNK)
            key_blk = (off + lax.broadcasted_iota(jnp.int32, (KV_CHUNK, LANES), 0)) >> SEL_SHIFT
            ln = lax.broadcasted_iota(jnp.int32, (KV_CHUNK, LANES), 1)
            ks = ks_ref[0, pl.ds(off, KV_CHUNK), :].astype(F32)
            for g in range(NSA_KV_HEADS):
                onehot = (ln - (1 - g) * HALF == key_blk).astype(F32)
                kaug_ref[g, pl.ds(off, KV_CHUNK), :] = jnp.where((ln >> HALF_SHIFT) == g, ks, onehot).astype(BF16)
            for j in range(sub):
                blk = pl.ds(pl.multiple_of(off + j * LANES, LANES), LANES)
                cols = slice(j * LANES, (j + 1) * LANES)
                vs_t = vs_ref[0, blk, :].astype(F32).T.astype(BF16)
                vw_t = vw_ref[0, blk, :].astype(F32).T.astype(BF16)
                for g in range(NSA_KV_HEADS):
                    vst_ref[c, g, 0:HEAD_DIM, cols] = vs_t[g * HALF:(g + 1) * HALF]
                    vwt_ref[c * sub + j, g, 0:HEAD_DIM, :] = vw_t[g * HALF:(g + 1) * HALF]
            return carry
        lax.fori_loop(0, s_len // KV_CHUNK, build, 0)
        for g in range(NSA_KV_HEADS):
            for j in range(n_cmp // LANES):
                cols = slice(j * LANES, (j + 1) * LANES)
                vct_ref[g, 0:HEAD_DIM, cols] = vc_ref[0, cols, :].astype(F32).T[g * HALF:(g + 1) * HALF].astype(BF16)
        ones = lambda ref: jnp.ones(ref.shape[:-2] + (V_ROWS - HEAD_DIM, ref.shape[-1]), BF16)
        vst_ref[:, :, HEAD_DIM:V_ROWS, :] = ones(vst_ref)
        vwt_ref[:, :, HEAD_DIM:V_ROWS, :] = ones(vwt_ref)
        vct_ref[:, HEAD_DIM:V_ROWS, :] = ones(vct_ref)

    q = q_ref[0]
    q_t = jnp.concatenate([q[:, c * LANES:(c + 1) * LANES].T for c in range(q.shape[1] // LANES)], axis=0)
    g_t = gate_ref[0].T
    tpos = start + lax.broadcasted_iota(jnp.int32, (1, Q_TILE), 1)
    j_idx = lax.broadcasted_iota(jnp.int32, (n_sel, 1), 0)
    zeros_half = jnp.zeros((HALF, rows), F32)
    halves = lambda g, own, other: jnp.concatenate([own, other] if g == 0 else [other, own], axis=0).astype(BF16)
    per_head = lambda a: jnp.concatenate([a] * NSA_GROUP, axis=1)
    key_iota = lambda n: lax.broadcasted_iota(jnp.int32, (n, 1), 0)
    cmp_valid = key_iota(n_cmp) * CMP_STRIDE + (CMP_LEN - 1) <= tpos
    cmp_bias = per_head(jnp.where(cmp_valid, 0.0, NEG_INF))
    cmp_keep = per_head(jnp.where(cmp_valid, 1.0, 0.0))

    q0, qa, o_cmp = [], [], []
    for g in range(NSA_KV_HEADS):
        h0 = NSA_GROUP * g
        qg = jnp.concatenate([q_t[(h0 + r) * HEAD_DIM:(h0 + r + 1) * HEAD_DIM, :] for r in range(NSA_GROUP)], axis=1)
        q0.append(halves(g, qg, zeros_half))

        s = _dot(kc_ref[0], q0[g])
        e = _masked_exp_t(s, cmp_bias) * cmp_keep
        nd = _dot(vct_ref[g], e.astype(BF16))
        inv = _safe_reciprocal(nd[HEAD_DIM:HEAD_DIM + 1])
        o_cmp.append(nd[0:HEAD_DIM] * inv)
        p = e * inv
        p_sum = p[:, 0:Q_TILE]
        for r in range(1, NSA_GROUP):
            p_sum = p_sum + p[:, r * Q_TILE:(r + 1) * Q_TILE]
        imp = jnp.dot(cover_ref[...], p_sum, precision=lax.Precision.HIGHEST,
                      preferred_element_type=F32)

        cur = tpos >> SEL_SHIFT
        forced = (j_idx == 0) | (j_idx == cur) | (j_idx == cur - 1)
        score = jnp.where(forced, FORCE_SCORE, jnp.where(j_idx * SEL_LEN <= tpos, imp, -FORCE_SCORE))
        strips = [score[b * 8:(b + 1) * 8] for b in range(n_sel // 8)]
        ranks = [jnp.zeros((8, Q_TILE), jnp.int32) for _ in strips]
        for jp in range(n_sel):
            other = score[jp:jp + 1, :]
            for b, strip in enumerate(strips):
                if jp < b * 8:
                    beats = other >= strip
                elif jp >= (b + 1) * 8:
                    beats = other > strip
                else:
                    beats = (other > strip) | ((other == strip) & (jp < j_idx[b * 8:(b + 1) * 8]))
                ranks[b] = ranks[b] + beats.astype(jnp.int32)
        rank = jnp.concatenate(ranks, axis=0)
        bias = jnp.where(rank < min(SEL_TOP_N, n_sel), 0.0, NEG_INF)
        if n_sel < HALF:
            bias = jnp.concatenate([bias, jnp.zeros((HALF - n_sel, Q_TILE), F32)], axis=0)
        qa.append(halves(g, qg, jnp.concatenate([bias] * NSA_GROUP, axis=1)))

    n_blk = (WINDOW + Q_TILE) // LANES
    wb = jnp.maximum(i * (Q_TILE // LANES) - WINDOW // LANES, 0)
    ws = pl.multiple_of(wb * LANES, LANES)
    dist = tpos - (ws + key_iota(n_blk * LANES))
    win_bias = per_head(jnp.where((dist >= 0) & (dist < WINDOW), 0.0, NEG_INF))
    mixed = []
    for g in range(NSA_KV_HEADS):
        sw = _dot(kw_ref[0, pl.ds(ws, n_blk * LANES), :], q0[g])
        ew = _masked_exp_t(sw, win_bias).astype(BF16)
        nd = _dot(vwt_ref[wb, g], ew[0:LANES])
        for j in range(1, n_blk):
            nd = nd + _dot(vwt_ref[wb + j, g], ew[j * LANES:(j + 1) * LANES])
        o_win = nd[0:HEAD_DIM] * (1.0 / nd[HEAD_DIM:HEAD_DIM + 1])
        per_r = []
        for r in range(NSA_GROUP):
            sl = slice(r * Q_TILE, (r + 1) * Q_TILE)
            c0 = (NSA_GROUP * g + r) * N_BRANCHES
            per_r.append(g_t[c0:c0 + 1] * o_cmp[g][:, sl] + g_t[c0 + 2:c0 + 3] * o_win[:, sl])
        mixed.append(per_r)

    diag = lax.div(i, KV_CHUNK // Q_TILE)

    def scores(g, c):
        off = pl.multiple_of(c * KV_CHUNK, KV_CHUNK)
        return _dot(kaug_ref[g, pl.ds(off, KV_CHUNK), :], qa[g])

    sbuf = (s0_ref, s1_ref)
    strip_max = lambda s: jnp.max(s.reshape(KV_CHUNK // 8, 8, rows), axis=0)
    causal_bias = per_head(jnp.where(diag * KV_CHUNK + key_iota(KV_CHUNK) <= tpos, 0.0, NEG_INF))

    def qk_stage(g, c, masked=False):
        s = scores(g, c)
        if masked:
            s = s + causal_bias
        sbuf[g][...] = s
        return strip_max(s)

    def pv_stage(g, c, m8, state):
        m, acc = state
        m_new = jnp.maximum(m, jnp.max(m8, axis=0, keepdims=True))
        p = jnp.exp2(sbuf[g][...] - m_new).astype(BF16)
        return m_new, jnp.exp2(m - m_new) * acc + _dot(vst_ref[c, g], p)

    def sel_chunk(c, carry):
        m8_0, st0, st1 = carry
        m8_1 = qk_stage(1, c)
        st0 = pv_stage(0, c, m8_0, st0)
        m8_0 = qk_stage(0, c + 1)
        st1 = pv_stage(1, c, m8_1, st1)
        return m8_0, st0, st1

    init = (jnp.full((1, rows), NEG_INF, F32), jnp.zeros((V_ROWS, rows), F32))
    _, st0, st1 = lax.fori_loop(0, diag, sel_chunk, (qk_stage(0, 0), init, init))
    s_d0 = s0_ref[...] + causal_bias
    s0_ref[...] = s_d0
    m8_1 = qk_stage(1, diag, masked=True)
    acc_sel = (pv_stage(0, diag, strip_max(s_d0), st0)[1], pv_stage(1, diag, m8_1, st1)[1])

    heads_out = []
    for g in range(NSA_KV_HEADS):
        acc = acc_sel[g]
        o_sel = acc[0:HEAD_DIM] * (1.0 / acc[HEAD_DIM:HEAD_DIM + 1])
        for r in range(NSA_GROUP):
            sl = slice(r * Q_TILE, (r + 1) * Q_TILE)
            gate = g_t[(NSA_GROUP * g + r) * N_BRANCHES + 1:(NSA_GROUP * g + r) * N_BRANCHES + 2]
            heads_out.append(mixed[g][r] + gate * o_sel[:, sl])
    for c in range(len(heads_out) // 2):
        pair = jnp.concatenate(heads_out[2 * c:2 * c + 2], axis=0)
        o_ref[0, :, c * LANES:(c + 1) * LANES] = pair.T.astype(o_ref.dtype)


def _cover_matrix_t(n_sel, n_cmp_padded):
    jj = np.arange(n_sel)[:, None]
    kk = np.arange(n_cmp_padded)[None, :]
    cover = ((kk * CMP_STRIDE < (jj + 1) * SEL_LEN) & (kk * CMP_STRIDE + CMP_LEN > jj * SEL_LEN)
             & (kk < n_cmp_padded - 1))
    return jnp.asarray(cover, F32)


def _nsa(q, gates, kcc, vcc, ks, vs, kw, vw):
    b_, s_, qd = q.shape
    n_cmp = kcc.shape[1]
    n_sel = s_ // SEL_LEN
    assert Q_TILE % LANES == 0 and KV_CHUNK % Q_TILE == 0 and n_sel <= HALF and n_sel % 8 == 0
    assert n_cmp % LANES == 0 and s_ % KV_CHUNK == 0 and s_ >= WINDOW + Q_TILE and WINDOW % LANES == 0
    cover = _cover_matrix_t(n_sel, n_cmp)
    tok = lambda w: pl.BlockSpec((1, Q_TILE, w), lambda b, i: (b, i, 0))
    seq = lambda n: pl.BlockSpec((1, n, LANES), lambda b, i: (b, 0, 0))
    return pl.pallas_call(
        _nsa_kernel,
        grid=(b_, s_ // Q_TILE),
        in_specs=[tok(qd), tok(LANES), seq(n_cmp), seq(n_cmp), seq(s_), seq(s_), seq(s_), seq(s_),
                  _const_spec(cover.shape)],
        out_specs=tok(qd),
        out_shape=jax.ShapeDtypeStruct((b_, s_, qd), BF16),
        scratch_shapes=[pltpu.VMEM((NSA_KV_HEADS, s_, LANES), BF16),
                        pltpu.VMEM((s_ // KV_CHUNK, NSA_KV_HEADS, V_ROWS, KV_CHUNK), BF16),
                        pltpu.VMEM((s_ // LANES, NSA_KV_HEADS, V_ROWS, LANES), BF16),
                        pltpu.VMEM((NSA_KV_HEADS, V_ROWS, n_cmp), BF16),
                        pltpu.VMEM((KV_CHUNK, NSA_GROUP * Q_TILE), F32),
                        pltpu.VMEM((KV_CHUNK, NSA_GROUP * Q_TILE), F32)],
        compiler_params=_params(2, VMEM_LIMIT),
        name="nsa",
    )(q, gates, kcc, vcc, ks, vs, kw, vw, cover)


def _swiglu_block(y, shift, scale, gate, w_in_ref, w_out_ref, ln_g, ln_b):
    h = (y * (1.0 + scale) + shift).astype(BF16)
    hid = w_out_ref.shape[0]
    acc = jnp.zeros(y.shape, F32)
    for c in range(hid // FFN_CHUNK):
        lo = c * FFN_CHUNK
        g_ = _dot(h, w_in_ref[:, lo:lo + FFN_CHUNK])
        u_ = _dot(h, w_in_ref[:, hid + lo:hid + lo + FFN_CHUNK])
        acc = acc + _dot((jax.nn.silu(g_) * u_).astype(BF16), w_out_ref[lo:lo + FFN_CHUNK, :])
    return _layer_norm(ALPHA * y + (1.0 + gate) * acc, ln_g, ln_b)


def _mix0_kernel(x_ref, on_ref, og_ref, gate1_ref, wo_ref, ln1g_ref, ln1b_ref,
                 shift2_ref, scale2_ref, gate2_ref, fin_ref, fout_ref, ln2g_ref, ln2b_ref, o_ref):
    half = on_ref.shape[2]
    out = _dot(on_ref[0], wo_ref[0:half, :]) + _dot(og_ref[0], wo_ref[half:2 * half, :])
    y = _layer_norm(ALPHA * x_ref[0] + (1.0 + gate1_ref[0]) * out, ln1g_ref[...], ln1b_ref[...])
    o_ref[0] = _swiglu_block(y, shift2_ref[0], scale2_ref[0], gate2_ref[0], fin_ref, fout_ref,
                             ln2g_ref[...], ln2b_ref[...])


def _mix0(x, o_nsa, o_gmlp, mod, w_out, ln_g, ln_b, ffn_w_in, ffn_w_out):
    b_, s_, d = x.shape
    tm = ROW_TILE
    wo = w_out.astype(BF16)
    fin = ffn_w_in.astype(BF16)
    fout = ffn_w_out.astype(BF16)
    row = lambda v: v.reshape(1, d)
    tok = lambda w: pl.BlockSpec((1, tm, w), lambda b, i: (b, i, 0))
    return pl.pallas_call(
        _mix0_kernel,
        grid=(b_, s_ // tm),
        in_specs=[tok(d), tok(o_nsa.shape[2]), tok(o_gmlp.shape[2]), _mod_spec(0, 2, b_, d),
                  _const_spec(wo.shape), _const_spec((1, d)), _const_spec((1, d)),
                  _mod_spec(1, 0, b_, d), _mod_spec(1, 1, b_, d), _mod_spec(1, 2, b_, d),
                  _const_spec(fin.shape), _const_spec(fout.shape), _const_spec((1, d)), _const_spec((1, d))],
        out_specs=tok(d),
        out_shape=jax.ShapeDtypeStruct(x.shape, F32),
        compiler_params=_params(2, VMEM_LIMIT),
        name="mix0",
    )(x, o_nsa, o_gmlp, mod, wo, row(ln_g[0]), row(ln_b[0]), mod, mod, mod, fin, fout, row(ln_g[1]), row(ln_b[1]))


def _layer1_kernel(x_ref, shift1_ref, scale1_ref, gate1_ref, win_ref, cw_ref, wo_ref, ln1g_ref, ln1b_ref,
                   shift2_ref, scale2_ref, gate2_ref, fin_ref, fout_ref, ln2g_ref, ln2b_ref, o_ref, carry_ref):
    i = pl.program_id(1)
    tm, d = x_ref.shape[1], x_ref.shape[2]
    halo = carry_ref.shape[0]

    @pl.when(i == 0)
    def _():
        carry_ref[...] = jnp.zeros(carry_ref.shape, F32)

    x = x_ref[0]
    h = (x * (1.0 + scale1_ref[0]) + shift1_ref[0]).astype(BF16)
    b_gate = _dot(h, win_ref[:, 0:d])
    cz = _dot(h, win_ref[:, d:2 * d]) * _dot(h, win_ref[:, 2 * d:3 * d])
    prev = carry_ref[...]
    carry_ref[...] = cz[tm - halo:tm]
    row = lax.broadcasted_iota(jnp.int32, (tm, 1), 0)
    back1 = jnp.where(row == 0, prev[halo - 1:halo], pltpu.roll(cz, 1, axis=0))
    back2 = jnp.where(row == 0, prev[halo - 2:halo - 1],
                      jnp.where(row == 1, prev[halo - 1:halo], pltpu.roll(cz, 2, axis=0)))
    cw = cw_ref[...]
    y = cw[0:1] * back2 + cw[1:2] * back1 + cw[2:3] * cz
    out = _dot((b_gate * y).astype(BF16), wo_ref[...])
    x1 = _layer_norm(ALPHA * x + (1.0 + gate1_ref[0]) * out, ln1g_ref[...], ln1b_ref[...])
    o_ref[0] = _swiglu_block(x1, shift2_ref[0], scale2_ref[0], gate2_ref[0], fin_ref, fout_ref,
                             ln2g_ref[...], ln2b_ref[...])


def _layer1(x, mod, w_in, conv_w, w_out, ln_g, ln_b, ffn_w_in, ffn_w_out):
    b_, s_, d = x.shape
    tm = ROW_TILE
    assert conv_w.shape[0] == CONV_WIDTH == 3
    win = w_in.astype(BF16)
    wo = w_out.astype(BF16)
    fin = ffn_w_in.astype(BF16)
    fout = ffn_w_out.astype(BF16)
    row = lambda v: v.reshape(1, d)
    tok = pl.BlockSpec((1, tm, d), lambda b, i: (b, i, 0))
    return pl.pallas_call(
        _layer1_kernel,
        grid=(b_, s_ // tm),
        in_specs=[tok, _mod_spec(2, 0, b_, d), _mod_spec(2, 1, b_, d), _mod_spec(2, 2, b_, d),
                  _const_spec(win.shape), _const_spec(conv_w.shape), _const_spec(wo.shape),
                  _const_spec((1, d)), _const_spec((1, d)),
                  _mod_spec(3, 0, b_, d), _mod_spec(3, 1, b_, d), _mod_spec(3, 2, b_, d),
                  _const_spec(fin.shape), _const_spec(fout.shape), _const_spec((1, d)), _const_spec((1, d))],
        out_specs=tok,
        out_shape=jax.ShapeDtypeStruct(x.shape, F32),
        scratch_shapes=[pltpu.VMEM((8, d), F32)],
        compiler_params=_params(2, VMEM_LIMIT),
        name="layer1",
    )(x, mod, mod, mod, win, conv_w, wo, row(ln_g[0]), row(ln_b[0]), mod, mod, mod, fin, fout,
      row(ln_g[1]), row(ln_b[1]))


def kernel(x, c, ada_w, ada_b, ln_g, ln_b, even_w_in, even_cmp_pos, even_cmp_w1, even_cmp_w2, even_gmlp_norm_g,
           even_gmlp_ws, even_gmlp_bs, even_w_out, odd_w_in, odd_conv_w, odd_w_out, ffn_w_in, ffn_w_out):
    assert ada_w.shape[0] == DEPTH == 2
    mod = _ada_modulation(c, ada_w, ada_b)
    q, kc, vc, ks, vs, kw, vw, gates, o_gmlp = _inproj0(
        x, mod, even_w_in[0], even_gmlp_norm_g[0], even_gmlp_ws[0], even_gmlp_bs[0])
    kcc, vcc = _compress(kc, vc, even_cmp_pos[0], even_cmp_w1[0], even_cmp_w2[0])
    o_nsa = _nsa(q, gates, kcc, vcc, ks, vs, kw, vw)
    x = _mix0(x, o_nsa, o_gmlp, mod, even_w_out[0], ln_g[0], ln_b[0], ffn_w_in[0], ffn_w_out[0])
    return _layer1(x, mod, odd_w_in[0], odd_conv_w[0], odd_w_out[0], ln_g[1], ln_b[1], ffn_w_in[1], ffn_w_out[1])
```

```python
import numpy as np
import jax
import jax.numpy as jnp
from jax import lax
from jax.experimental import pallas as pl
from jax.experimental.pallas import tpu as pltpu

HEAD_DIM = 64
NSA_Q_HEADS = 8
NSA_KV_HEADS = 2
NSA_GROUP = NSA_Q_HEADS // NSA_KV_HEADS
CMP_LEN = 32
CMP_STRIDE = 16
CMP_HIDDEN = 128
SEL_LEN = 64
SEL_TOP_N = 16
WINDOW = 512
N_BRANCHES = 3
GMLP_GROUPS = 8
GMLP_HEAD_DIM = 64
GMLP_CHUNK = 128
CONV_WIDTH = 3
DEPTH = 2
ALPHA = (2 * DEPTH) ** 0.25
LN_EPS = 1e-5
NEG_INF = -1e30
FORCE_SCORE = 1e4

LANES = 128
HALF = LANES // 2
HALF_SHIFT = 6
SEL_SHIFT = 6
Q_TILE = 256
KV_CHUNK = 512
V_ROWS = HEAD_DIM + 16
LOG2_E = 1.4426950408889634
ROW_TILE = 512
FFN_CHUNK = 256
VMEM_LIMIT = 56 * 1024 * 1024

F32 = jnp.float32
BF16 = jnp.bfloat16


def _dot(a, b):
    return jnp.dot(a, b, preferred_element_type=F32)


def _layer_norm(v, g, b):
    mu = jnp.mean(v, axis=-1, keepdims=True)
    d = v - mu
    var = jnp.mean(d * d, axis=-1, keepdims=True)
    return d * lax.rsqrt(var + LN_EPS) * g + b


def _const_spec(shape):
    zeros = (0,) * len(shape)
    return pl.BlockSpec(shape, lambda *_: zeros, pipeline_mode=pl.Buffered(1))


def _params(n_axes, vmem=None):
    return pltpu.CompilerParams(dimension_semantics=("arbitrary",) * n_axes, vmem_limit_bytes=vmem)


def _ada_kernel(c_ref, w_ref, b_ref, o_ref):
    a = jax.nn.silu(c_ref[...]).astype(BF16)
    o_ref[0] = _dot(a, w_ref[0].astype(BF16)) + b_ref[0]


def _ada_modulation(c, ada_w, ada_b):
    b_, d = c.shape
    n_pairs = ada_w.shape[0] * ada_w.shape[1]
    w = ada_w.reshape(n_pairs, d, 3 * d)
    bias = ada_b.reshape(n_pairs, 1, 3 * d)
    mod = pl.pallas_call(
        _ada_kernel,
        grid=(n_pairs, 3),
        in_specs=[
            pl.BlockSpec((b_, d), lambda i, j: (0, 0)),
            pl.BlockSpec((1, d, d), lambda i, j: (i, 0, j)),
            pl.BlockSpec((1, 1, d), lambda i, j: (i, 0, j)),
        ],
        out_specs=pl.BlockSpec((1, b_, d), lambda i, j: (i, 0, j)),
        out_shape=jax.ShapeDtypeStruct((n_pairs, b_, 3 * d), F32),
        compiler_params=_params(2),
        name="ada",
    )(c, w, bias)
    return mod.reshape(n_pairs * b_ * 3, 1, d)


def _mod_spec(pair, which, b_, d):
    return pl.BlockSpec((1, 1, d), lambda b, i: ((pair * b_ + b) * 3 + which, 0, 0))


def _segment_sum(x, seg):
    hi = x.astype(BF16)
    lo = (x - hi.astype(F32)).astype(BF16)
    return _dot(hi, seg) + _dot(lo, seg)


def _inproj0_kernel(x_ref, shift_ref, scale_ref, wq_ref, wkv_ref, wu_ref, wv_ref, seg_ref, ng_ref,
                    wsp_ref, bsp_ref,
                    q_ref, kc_ref, vc_ref, ks_ref, vs_ref, kw_ref, vw_ref, gate_ref, og_ref):
    tm = x_ref.shape[1]
    h = (x_ref[0] * (1.0 + scale_ref[0]) + shift_ref[0]).astype(BF16)
    q_ref[0] = _dot(h, wq_ref[...]) * (HEAD_DIM ** -0.5 * LOG2_E)
    kv = _dot(h, wkv_ref[...])
    for n, ref in enumerate((kc_ref, vc_ref, ks_ref, vs_ref, kw_ref, vw_ref)):
        ref[0] = kv[:, n * LANES:(n + 1) * LANES].astype(ref.dtype)
    gate_ref[0] = jax.nn.sigmoid(kv[:, 6 * LANES:7 * LANES])

    u = jax.nn.gelu(_dot(h, wu_ref[...]))
    vf = jax.nn.gelu(_dot(h, wv_ref[...]))
    seg = seg_ref[...]
    inv = 1.0 / GMLP_HEAD_DIM
    mu = _segment_sum(vf, seg) * inv
    dv = vf - mu
    var = _segment_sum(dv * dv, seg) * inv
    vn = dv * lax.rsqrt(var + LN_EPS) * ng_ref[...]

    left = lax.broadcasted_iota(jnp.int32, (GMLP_CHUNK, LANES), 1) < HALF
    t_idx = lax.broadcasted_iota(jnp.int32, (GMLP_CHUNK, 2 * GMLP_CHUNK), 0)
    s_idx = lax.broadcasted_iota(jnp.int32, (GMLP_CHUNK, 2 * GMLP_CHUNK), 1) & (GMLP_CHUNK - 1)
    causal = s_idx <= t_idx
    for p in range(GMLP_GROUPS // 2):
        cols = slice(p * LANES, (p + 1) * LANES)
        wcat = jnp.where(causal, wsp_ref[p], 0.0).astype(BF16)
        bias = bsp_ref[:, cols]
        for c in range(tm // GMLP_CHUNK):
            rows = slice(c * GMLP_CHUNK, (c + 1) * GMLP_CHUNK)
            vp = vn[rows, cols]
            v2 = jnp.concatenate([jnp.where(left, vp, 0.0), jnp.where(left, 0.0, vp)], axis=0).astype(BF16)
            mixed = _dot(wcat, v2)
            og_ref[0, rows, cols] = (u[rows, cols] * (mixed + bias)).astype(og_ref.dtype)


def _inproj0(x, mod, w_in, norm_g, w_s, b_s):
    b_, s_, d = x.shape
    tm = ROW_TILE
    qd = NSA_Q_HEADS * HEAD_DIM
    kvd = NSA_KV_HEADS * HEAD_DIM
    gd = NSA_Q_HEADS * N_BRANCHES
    gm = GMLP_GROUPS * GMLP_HEAD_DIM
    o_kv = qd
    o_g = qd + 6 * kvd
    o_u = o_g + gd
    o_v = o_u + gm
    wq = w_in[:, :qd].astype(BF16)
    wkv = jnp.concatenate([w_in[:, o_kv:o_g], w_in[:, o_g:o_u],
                           jnp.zeros((d, LANES - gd), w_in.dtype)], axis=1).astype(BF16)
    wu = w_in[:, o_u:o_v].astype(BF16)
    wv = w_in[:, o_v:o_v + gm].astype(BF16)
    seg = jnp.asarray(np.kron(np.eye(GMLP_GROUPS), np.ones((GMLP_HEAD_DIM, GMLP_HEAD_DIM))), BF16)
    ng = norm_g.reshape(1, gm)
    wsp = w_s.reshape(GMLP_GROUPS // 2, 2, GMLP_CHUNK, GMLP_CHUNK).transpose(0, 2, 1, 3).reshape(
        GMLP_GROUPS // 2, GMLP_CHUNK, 2 * GMLP_CHUNK)
    bsp = jnp.broadcast_to(b_s.T[:, :, None], (GMLP_CHUNK, GMLP_GROUPS, GMLP_HEAD_DIM)).reshape(GMLP_CHUNK, gm)

    tok = lambda w: pl.BlockSpec((1, tm, w), lambda b, i: (b, i, 0))
    shp = lambda w, dt: jax.ShapeDtypeStruct((b_, s_, w), dt)
    return pl.pallas_call(
        _inproj0_kernel,
        grid=(b_, s_ // tm),
        in_specs=[tok(d), _mod_spec(0, 0, b_, d), _mod_spec(0, 1, b_, d),
                  _const_spec(wq.shape), _const_spec(wkv.shape), _const_spec(wu.shape), _const_spec(wv.shape),
                  _const_spec(seg.shape), _const_spec(ng.shape), _const_spec(wsp.shape), _const_spec(bsp.shape)],
        out_specs=[tok(qd)] + [tok(LANES)] * 7 + [tok(gm)],
        out_shape=[shp(qd, F32), shp(LANES, F32), shp(LANES, F32)] + [shp(LANES, BF16)] * 4
                  + [shp(LANES, F32), shp(gm, BF16)],
        compiler_params=_params(2, VMEM_LIMIT),
        name="inproj0",
    )(x, mod, mod, wq, wkv, wu, wv, seg, ng, wsp, bsp)


def _compress_kernel(kch_ref, vch_ref, pos_ref, w1a_ref, w1b_ref, w2_ref, kco_ref, vco_ref):
    for t, (src, dst) in enumerate(((kch_ref, kco_ref), (vch_ref, vco_ref))):
        ch = src[0]
        n = ch.shape[0]
        first = (ch + pos_ref[t, 0:1]).astype(BF16)
        second = (ch + pos_ref[t, 1:2]).astype(BF16)
        ha = _dot(first, w1a_ref[t])
        hb = _dot(second, w1b_ref[t])
        h1 = ha + pltpu.roll(hb, n - 1, axis=0)
        dst[0] = _dot(jax.nn.gelu(h1).astype(BF16), w2_ref[t]).astype(dst.dtype)


def _compress(kc, vc, cmp_pos, cmp_w1, cmp_w2):
    b_, s_, _ = kc.shape
    nch = s_ // CMP_STRIDE
    r = CMP_LEN // CMP_STRIDE
    assert r == 2
    eye = jnp.eye(NSA_KV_HEADS, dtype=cmp_w1.dtype)
    w1 = cmp_w1.reshape(2, r, CMP_STRIDE, HEAD_DIM, CMP_HIDDEN)
    w1 = jnp.einsum("thldc,gk->thlgdkc", w1, eye).reshape(
        2, r, CMP_STRIDE * LANES, NSA_KV_HEADS * CMP_HIDDEN).astype(BF16)
    w2 = jnp.einsum("tcd,gk->tgckd", cmp_w2, eye).reshape(2, NSA_KV_HEADS * CMP_HIDDEN, LANES).astype(BF16)
    pos = jnp.broadcast_to(cmp_pos.reshape(2, r, CMP_STRIDE, 1, HEAD_DIM),
                           (2, r, CMP_STRIDE, NSA_KV_HEADS, HEAD_DIM)).reshape(2, r, CMP_STRIDE * LANES)
    kch = kc.reshape(b_, nch, CMP_STRIDE * LANES)
    vch = vc.reshape(b_, nch, CMP_STRIDE * LANES)
    ch_spec = pl.BlockSpec((1, nch, CMP_STRIDE * LANES), lambda b: (b, 0, 0))
    out_spec = pl.BlockSpec((1, nch, LANES), lambda b: (b, 0, 0))
    out_shape = jax.ShapeDtypeStruct((b_, nch, LANES), BF16)
    return pl.pallas_call(
        _compress_kernel,
        grid=(b_,),
        in_specs=[ch_spec, ch_spec, _const_spec(pos.shape), _const_spec(w1[:, 0].shape),
                  _const_spec(w1[:, 1].shape), _const_spec(w2.shape)],
        out_specs=[out_spec, out_spec],
        out_shape=[out_shape, out_shape],
        compiler_params=_params(1, VMEM_LIMIT),
        name="compress",
    )(kch, vch, pos, w1[:, 0], w1[:, 1], w2)


def _nsa_kernel(q_ref, gate_ref, kc_ref, vc_ref, ks_ref, vs_ref, kw_ref, vw_ref, cover_ref, o_ref,
                kaug_ref, vst_ref, vwt_ref, vct_ref, s0_ref, s1_ref, c0_ref, c1_ref, w0_ref, w1_ref):
    i = pl.program_id(1)
    start = i * Q_TILE
    s_len = ks_ref.shape[1]
    n_sel = s_len // SEL_LEN
    n_cmp = kc_ref.shape[1]
    rows = NSA_GROUP * Q_TILE
    sub = KV_CHUNK // LANES

    @pl.when(i == 0)
    def _():
        def build(c, carry):
            off = pl.multiple_of(c * KV_CHUNK, KV_CHUNK)
            key_blk = (off + lax.broadcasted_iota(jnp.int32, (KV_CHUNK, LANES), 0)) >> SEL_SHIFT
            ln = lax.broadcasted_iota(jnp.int32, (KV_CHUNK, LANES), 1)
            ks = ks_ref[0, pl.ds(off, KV_CHUNK), :].astype(F32)
            for g in range(NSA_KV_HEADS):
                onehot = (ln - (1 - g) * HALF == key_blk).astype(F32)
                kaug_ref[g, pl.ds(off, KV_CHUNK), :] = jnp.where((ln >> HALF_SHIFT) == g, ks, onehot).astype(BF16)
            for j in range(sub):
                blk = pl.ds(pl.multiple_of(off + j * LANES, LANES), LANES)
                cols = slice(j * LANES, (j + 1) * LANES)
                vs_t = vs_ref[0, blk, :].astype(F32).T.astype(BF16)
                vw_t = vw_ref[0, blk, :].astype(F32).T.astype(BF16)
                for g in range(NSA_KV_HEADS):
                    vst_ref[c, g, 0:HEAD_DIM, cols] = vs_t[g * HALF:(g + 1) * HALF]
                    vwt_ref[c * sub + j, g, 0:HEAD_DIM, :] = vw_t[g * HALF:(g + 1) * HALF]
            return carry
        lax.fori_loop(0, s_len // KV_CHUNK, build, 0)
        for g in range(NSA_KV_HEADS):
            for j in range(n_cmp // LANES):
                cols = slice(j * LANES, (j + 1) * LANES)
                vct_ref[g, 0:HEAD_DIM, cols] = vc_ref[0, cols, :].astype(F32).T[g * HALF:(g + 1) * HALF].astype(BF16)
        ones = lambda ref: jnp.ones(ref.shape[:-2] + (V_ROWS - HEAD_DIM, ref.shape[-1]), BF16)
        vst_ref[:, :, HEAD_DIM:V_ROWS, :] = ones(vst_ref)
        vwt_ref[:, :, HEAD_DIM:V_ROWS, :] = ones(vwt_ref)
        vct_ref[:, HEAD_DIM:V_ROWS, :] = ones(vct_ref)

    q = q_ref[0]
    q_t = jnp.concatenate([q[:, c * LANES:(c + 1) * LANES].T for c in range(q.shape[1] // LANES)], axis=0)
    g_t = gate_ref[0].T
    tpos = start + lax.broadcasted_iota(jnp.int32, (1, Q_TILE), 1)
    j_idx = lax.broadcasted_iota(jnp.int32, (n_sel, 1), 0)
    zeros_half = jnp.zeros((HALF, rows), F32)
    halves = lambda g, own, other: jnp.concatenate([own, other] if g == 0 else [other, own], axis=0).astype(BF16)
    per_head = lambda a: jnp.concatenate([a] * NSA_GROUP, axis=1)
    key_iota = lambda n: lax.broadcasted_iota(jnp.int32, (n, 1), 0)
    strip_max = lambda s: jnp.max(s.reshape(s.shape[0] // 8, 8, rows), axis=0)
    col_max = lambda m8: jnp.max(m8, axis=0, keepdims=True)

    qg = [jnp.concatenate([q_t[(NSA_GROUP * g + r) * HEAD_DIM:(NSA_GROUP * g + r + 1) * HEAD_DIM, :]
                           for r in range(NSA_GROUP)], axis=1) for g in range(NSA_KV_HEADS)]
    q0 = [halves(g, qg[g], zeros_half) for g in range(NSA_KV_HEADS)]
    cbuf, wbuf = (c0_ref, c1_ref), (w0_ref, w1_ref)

    cmp_bias = per_head(jnp.where(key_iota(n_cmp) * CMP_STRIDE + (CMP_LEN - 1) <= tpos, 0.0, NEG_INF))
    has_cmp_key = per_head(tpos >= CMP_LEN - 1)

    def cmp_qk(g):
        s = _dot(kc_ref[0], q0[g]) + cmp_bias
        cbuf[g][...] = s
        return strip_max(s)

    def cmp_pv(g, m8):
        e = jnp.exp2(cbuf[g][...] - col_max(m8))
        nd = _dot(vct_ref[g], e.astype(BF16))
        inv = jnp.where(has_cmp_key, 1.0 / nd[HEAD_DIM:HEAD_DIM + 1], 0.0)
        p = e * inv
        p_sum = p[:, 0:Q_TILE]
        for r in range(1, NSA_GROUP):
            p_sum = p_sum + p[:, r * Q_TILE:(r + 1) * Q_TILE]
        imp = jnp.dot(cover_ref[...], p_sum, precision=lax.Precision.HIGHEST,
                      preferred_element_type=F32)
        return nd[0:HEAD_DIM] * inv, imp

    def select(g, imp):
        cur = tpos >> SEL_SHIFT
        forced = (j_idx == 0) | (j_idx == cur) | (j_idx == cur - 1)
        score = jnp.where(forced, FORCE_SCORE, jnp.where(j_idx * SEL_LEN <= tpos, imp, -FORCE_SCORE))
        strips = [score[b * 8:(b + 1) * 8] for b in range(n_sel // 8)]
        ranks = [jnp.zeros((8, Q_TILE), jnp.int32) for _ in strips]
        for jp in range(n_sel):
            other = score[jp:jp + 1, :]
            for b, strip in enumerate(strips):
                if jp < b * 8:
                    beats = other >= strip
                elif jp >= (b + 1) * 8:
                    beats = other > strip
                else:
                    beats = (other > strip) | ((other == strip) & (jp < j_idx[b * 8:(b + 1) * 8]))
                ranks[b] = ranks[b] + beats.astype(jnp.int32)
        rank = jnp.concatenate(ranks, axis=0)
        bias = jnp.where(rank < min(SEL_TOP_N, n_sel), 0.0, NEG_INF)
        if n_sel < HALF:
            bias = jnp.concatenate([bias, jnp.zeros((HALF - n_sel, Q_TILE), F32)], axis=0)
        return halves(g, qg[g], per_head(bias))

    n_blk = (WINDOW + Q_TILE) // LANES
    wb = jnp.maximum(i * (Q_TILE // LANES) - WINDOW // LANES, 0)
    ws = pl.multiple_of(wb * LANES, LANES)
    dist = tpos - (ws + key_iota(n_blk * LANES))
    win_bias = per_head(jnp.where((dist >= 0) & (dist < WINDOW), 0.0, NEG_INF))

    def win_qk(g):
        s = _dot(kw_ref[0, pl.ds(ws, n_blk * LANES), :], q0[g]) + win_bias
        wbuf[g][...] = s
        return strip_max(s)

    def win_pv(g, m8):
        m = col_max(m8)
        nd = jnp.zeros((V_ROWS, rows), F32)
        for j in range(n_blk):
            blk = slice(j * LANES, (j + 1) * LANES)
            nd = nd + _dot(vwt_ref[wb + j, g], jnp.exp2(wbuf[g][blk, :] - m).astype(BF16))
        return nd[0:HEAD_DIM] * (1.0 / nd[HEAD_DIM:HEAD_DIM + 1])

    def mix(g, o_cmp, o_win):
        per_r = []
        for r in range(NSA_GROUP):
            sl = slice(r * Q_TILE, (r + 1) * Q_TILE)
            c0 = (NSA_GROUP * g + r) * N_BRANCHES
            per_r.append(g_t[c0:c0 + 1] * o_cmp[:, sl] + g_t[c0 + 2:c0 + 3] * o_win[:, sl])
        return per_r

    cm0 = cmp_qk(0)
    cm1 = cmp_qk(1)
    o_cmp0, imp0 = cmp_pv(0, cm0)
    wm0 = win_qk(0)
    o_cmp1, imp1 = cmp_pv(1, cm1)
    qa = [select(0, imp0)]
    wm1 = win_qk(1)
    mixed = [mix(0, o_cmp0, win_pv(0, wm0))]
    qa.append(select(1, imp1))
    mixed.append(mix(1, o_cmp1, win_pv(1, wm1)))

    diag = lax.div(i, KV_CHUNK // Q_TILE)

    def scores(g, c):
        off = pl.multiple_of(c * KV_CHUNK, KV_CHUNK)
        return _dot(kaug_ref[g, pl.ds(off, KV_CHUNK), :], qa[g])

    sbuf = (s0_ref, s1_ref)
    causal_bias = per_head(jnp.where(diag * KV_CHUNK + key_iota(KV_CHUNK) <= tpos, 0.0, NEG_INF))

    def qk_stage(g, c, masked=False):
        s = scores(g, c)
        if masked:
            s = s + causal_bias
        sbuf[g][...] = s
        return strip_max(s)

    def pv_stage(g, c, m8, state):
        m, acc = state
        m_new = jnp.maximum(m, col_max(m8))
        p = jnp.exp2(sbuf[g][...] - m_new).astype(BF16)
        return m_new, jnp.exp2(m - m_new) * acc + _dot(vst_ref[c, g], p)

    def sel_chunk(c, carry):
        m8_0, st0, st1 = carry
        m8_1 = qk_stage(1, c)
        st0 = pv_stage(0, c, m8_0, st0)
        m8_0 = qk_stage(0, c + 1)
        st1 = pv_stage(1, c, m8_1, st1)
        return m8_0, st0, st1

    init = (jnp.full((1, rows), NEG_INF, F32), jnp.zeros((V_ROWS, rows), F32))
    _, st0, st1 = lax.fori_loop(0, diag, sel_chunk, (qk_stage(0, 0), init, init))
    s_d0 = s0_ref[...] + causal_bias
    s0_ref[...] = s_d0
    m8_1 = qk_stage(1, diag, masked=True)
    acc_sel = (pv_stage(0, diag, strip_max(s_d0), st0)[1], pv_stage(1, diag, m8_1, st1)[1])

    heads_out = []
    for g in range(NSA_KV_HEADS):
        acc = acc_sel[g]
        o_sel = acc[0:HEAD_DIM] * (1.0 / acc[HEAD_DIM:HEAD_DIM + 1])
        for r in range(NSA_GROUP):
            sl = slice(r * Q_TILE, (r + 1) * Q_TILE)
            gate = g_t[(NSA_GROUP * g + r) * N_BRANCHES + 1:(NSA_GROUP * g + r) * N_BRANCHES + 2]
            heads_out.append(mixed[g][r] + gate * o_sel[:, sl])
    for c in range(len(heads_out) // 2):
        pair = jnp.concatenate(heads_out[2 * c:2 * c + 2], axis=0)
        o_ref[0, :, c * LANES:(c + 1) * LANES] = pair.T.astype(o_ref.dtype)


def _cover_matrix_t(n_sel, n_cmp_padded):
    jj = np.arange(n_sel)[:, None]
    kk = np.arange(n_cmp_padded)[None, :]
    cover = ((kk * CMP_STRIDE < (jj + 1) * SEL_LEN) & (kk * CMP_STRIDE + CMP_LEN > jj * SEL_LEN)
             & (kk < n_cmp_padded - 1))
    return jnp.asarray(cover, F32)


def _nsa(q, gates, kcc, vcc, ks, vs, kw, vw):
    b_, s_, qd = q.shape
    n_cmp = kcc.shape[1]
    n_sel = s_ // SEL_LEN
    assert Q_TILE % LANES == 0 and KV_CHUNK % Q_TILE == 0 and n_sel <= HALF and n_sel % 8 == 0
    assert n_cmp % LANES == 0 and s_ % KV_CHUNK == 0 and s_ >= WINDOW + Q_TILE and WINDOW % LANES == 0
    cover = _cover_matrix_t(n_sel, n_cmp)
    tok = lambda w: pl.BlockSpec((1, Q_TILE, w), lambda b, i: (b, i, 0))
    seq = lambda n: pl.BlockSpec((1, n, LANES), lambda b, i: (b, 0, 0))
    return pl.pallas_call(
        _nsa_kernel,
        grid=(b_, s_ // Q_TILE),
        in_specs=[tok(qd), tok(LANES), seq(n_cmp), seq(n_cmp), seq(s_), seq(s_), seq(s_), seq(s_),
                  _const_spec(cover.shape)],
        out_specs=tok(qd),
        out_shape=jax.ShapeDtypeStruct((b_, s_, qd), BF16),
        scratch_shapes=[pltpu.VMEM((NSA_KV_HEADS, s_, LANES), BF16),
                        pltpu.VMEM((s_ // KV_CHUNK, NSA_KV_HEADS, V_ROWS, KV_CHUNK), BF16),
                        pltpu.VMEM((s_ // LANES, NSA_KV_HEADS, V_ROWS, LANES), BF16),
                        pltpu.VMEM((NSA_KV_HEADS, V_ROWS, n_cmp), BF16),
                        pltpu.VMEM((KV_CHUNK, NSA_GROUP * Q_TILE), F32),
                        pltpu.VMEM((KV_CHUNK, NSA_GROUP * Q_TILE), F32),
                        pltpu.VMEM((n_cmp, NSA_GROUP * Q_TILE), F32),
                        pltpu.VMEM((n_cmp, NSA_GROUP * Q_TILE), F32),
                        pltpu.VMEM((WINDOW + Q_TILE, NSA_GROUP * Q_TILE), F32),
                        pltpu.VMEM((WINDOW + Q_TILE, NSA_GROUP * Q_TILE), F32)],
        compiler_params=_params(2, VMEM_LIMIT),
        name="nsa",
    )(q, gates, kcc, vcc, ks, vs, kw, vw, cover)


def _swiglu_block(y, shift, scale, gate, w_in_ref, w_out_ref, ln_g, ln_b):
    h = (y * (1.0 + scale) + shift).astype(BF16)
    hid = w_out_ref.shape[0]
    acc = jnp.zeros(y.shape, F32)
    for c in range(hid // FFN_CHUNK):
        lo = c * FFN_CHUNK
        g_ = _dot(h, w_in_ref[:, lo:lo + FFN_CHUNK])
        u_ = _dot(h, w_in_ref[:, hid + lo:hid + lo + FFN_CHUNK])
        acc = acc + _dot((jax.nn.silu(g_) * u_).astype(BF16), w_out_ref[lo:lo + FFN_CHUNK, :])
    return _layer_norm(ALPHA * y + (1.0 + gate) * acc, ln_g, ln_b)


def _mix0_kernel(x_ref, on_ref, og_ref, gate1_ref, wo_ref, ln1g_ref, ln1b_ref,
                 shift2_ref, scale2_ref, gate2_ref, fin_ref, fout_ref, ln2g_ref, ln2b_ref, o_ref):
    half = on_ref.shape[2]
    out = _dot(on_ref[0], wo_ref[0:half, :]) + _dot(og_ref[0], wo_ref[half:2 * half, :])
    y = _layer_norm(ALPHA * x_ref[0] + (1.0 + gate1_ref[0]) * out, ln1g_ref[...], ln1b_ref[...])
    o_ref[0] = _swiglu_block(y, shift2_ref[0], scale2_ref[0], gate2_ref[0], fin_ref, fout_ref,
                             ln2g_ref[...], ln2b_ref[...])


def _mix0(x, o_nsa, o_gmlp, mod, w_out, ln_g, ln_b, ffn_w_in, ffn_w_out):
    b_, s_, d = x.shape
    tm = ROW_TILE
    wo = w_out.astype(BF16)
    fin = ffn_w_in.astype(BF16)
    fout = ffn_w_out.astype(BF16)
    row = lambda v: v.reshape(1, d)
    tok = lambda w: pl.BlockSpec((1, tm, w), lambda b, i: (b, i, 0))
    return pl.pallas_call(
        _mix0_kernel,
        grid=(b_, s_ // tm),
        in_specs=[tok(d), tok(o_nsa.shape[2]), tok(o_gmlp.shape[2]), _mod_spec(0, 2, b_, d),
                  _const_spec(wo.shape), _const_spec((1, d)), _const_spec((1, d)),
                  _mod_spec(1, 0, b_, d), _mod_spec(1, 1, b_, d), _mod_spec(1, 2, b_, d),
                  _const_spec(fin.shape), _const_spec(fout.shape), _const_spec((1, d)), _const_spec((1, d))],
        out_specs=tok(d),
        out_shape=jax.ShapeDtypeStruct(x.shape, F32),
        compiler_params=_params(2, VMEM_LIMIT),
        name="mix0",
    )(x, o_nsa, o_gmlp, mod, wo, row(ln_g[0]), row(ln_b[0]), mod, mod, mod, fin, fout, row(ln_g[1]), row(ln_b[1]))


def _layer1_kernel(x_ref, shift1_ref, scale1_ref, gate1_ref, win_ref, cw_ref, wo_ref, ln1g_ref, ln1b_ref,
                   shift2_ref, scale2_ref, gate2_ref, fin_ref, fout_ref, ln2g_ref, ln2b_ref, o_ref, carry_ref):
    i = pl.program_id(1)
    tm, d = x_ref.shape[1], x_ref.shape[2]
    halo = carry_ref.shape[0]

    @pl.when(i == 0)
    def _():
        carry_ref[...] = jnp.zeros(carry_ref.shape, F32)

    x = x_ref[0]
    h = (x * (1.0 + scale1_ref[0]) + shift1_ref[0]).astype(BF16)
    b_gate = _dot(h, win_ref[:, 0:d])
    cz = _dot(h, win_ref[:, d:2 * d]) * _dot(h, win_ref[:, 2 * d:3 * d])
    prev = carry_ref[...]
    carry_ref[...] = cz[tm - halo:tm]
    row = lax.broadcasted_iota(jnp.int32, (tm, 1), 0)
    back1 = jnp.where(row == 0, prev[halo - 1:halo], pltpu.roll(cz, 1, axis=0))
    back2 = jnp.where(row == 0, prev[halo - 2:halo - 1],
                      jnp.where(row == 1, prev[halo - 1:halo], pltpu.roll(cz, 2, axis=0)))
    cw = cw_ref[...]
    y = cw[0:1] * back2 + cw[1:2] * back1 + cw[2:3] * cz
    out = _dot((b_gate * y).astype(BF16), wo_ref[...])
    x1 = _layer_norm(ALPHA * x + (1.0 + gate1_ref[0]) * out, ln1g_ref[...], ln1b_ref[...])
    o_ref[0] = _swiglu_block(x1, shift2_ref[0], scale2_ref[0], gate2_ref[0], fin_ref, fout_ref,
                             ln2g_ref[...], ln2b_ref[...])


def _layer1(x, mod, w_in, conv_w, w_out, ln_g, ln_b, ffn_w_in, ffn_w_out):
    b_, s_, d = x.shape
    tm = ROW_TILE
    assert conv_w.shape[0] == CONV_WIDTH == 3
    win = w_in.astype(BF16)
    wo = w_out.astype(BF16)
    fin = ffn_w_in.astype(BF16)
    fout = ffn_w_out.astype(BF16)
    row = lambda v: v.reshape(1, d)
    tok = pl.BlockSpec((1, tm, d), lambda b, i: (b, i, 0))
    return pl.pallas_call(
        _layer1_kernel,
        grid=(b_, s_ // tm),
        in_specs=[tok, _mod_spec(2, 0, b_, d), _mod_spec(2, 1, b_, d), _mod_spec(2, 2, b_, d),
                  _const_spec(win.shape), _const_spec(conv_w.shape), _const_spec(wo.shape),
                  _const_spec((1, d)), _const_spec((1, d)),
                  _mod_spec(3, 0, b_, d), _mod_spec(3, 1, b_, d), _mod_spec(3, 2, b_, d),
                  _const_spec(fin.shape), _const_spec(fout.shape), _const_spec((1, d)), _const_spec((1, d))],
        out_specs=tok,
        out_shape=jax.ShapeDtypeStruct(x.shape, F32),
        scratch_shapes=[pltpu.VMEM((8, d), F32)],
        compiler_params=_params(2, VMEM_LIMIT),
        name="layer1",
    )(x, mod, mod, mod, win, conv_w, wo, row(ln_g[0]), row(ln_b[0]), mod, mod, mod, fin, fout,
      row(ln_g[1]), row(ln_b[1]))


def kernel(x, c, ada_w, ada_b, ln_g, ln_b, even_w_in, even_cmp_pos, even_cmp_w1, even_cmp_w2, even_gmlp_norm_g,
           even_gmlp_ws, even_gmlp_bs, even_w_out, odd_w_in, odd_conv_w, odd_w_out, ffn_w_in, ffn_w_out):
    assert ada_w.shape[0] == DEPTH == 2
    mod = _ada_modulation(c, ada_w, ada_b)
    q, kc, vc, ks, vs, kw, vw, gates, o_gmlp = _inproj0(
        x, mod, even_w_in[0], even_gmlp_norm_g[0], even_gmlp_ws[0], even_gmlp_bs[0])
    kcc, vcc = _compress(kc, vc, even_cmp_pos[0], even_cmp_w1[0], even_cmp_w2[0])
    o_nsa = _nsa(q, gates, kcc, vcc, ks, vs, kw, vw)
    x = _mix0(x, o_nsa, o_gmlp, mod, even_w_out[0], ln_g[0], ln_b[0], ffn_w_in[0], ffn_w_out[0])
    return _layer1(x, mod, odd_w_in[0], odd_conv_w[0], odd_w_out[0], ln_g[1], ln_b[1], ffn_w_in[1], ffn_w_out[1])
```

```python
import numpy as np
import jax
import jax.numpy as jnp
from jax import lax
from jax.experimental import pallas as pl
from jax.experimental.pallas import tpu as pltpu

HEAD_DIM = 64
NSA_Q_HEADS = 8
NSA_KV_HEADS = 2
NSA_GROUP = NSA_Q_HEADS // NSA_KV_HEADS
CMP_LEN = 32
CMP_STRIDE = 16
CMP_HIDDEN = 128
SEL_LEN = 64
SEL_TOP_N = 16
WINDOW = 512
N_BRANCHES = 3
GMLP_GROUPS = 8
GMLP_HEAD_DIM = 64
GMLP_CHUNK = 128
CONV_WIDTH = 3
DEPTH = 2
ALPHA = (2 * DEPTH) ** 0.25
LN_EPS = 1e-5
NEG_INF = -1e30
FORCE_SCORE = 1e4

LANES = 128
HALF = LANES // 2
HALF_SHIFT = 6
SEL_SHIFT = 6
Q_TILE = 256
KV_CHUNK = 512
V_ROWS = HEAD_DIM + 16
LOG2_E = 1.4426950408889634
ROW_TILE = 512
FFN_CHUNK = 256
VMEM_LIMIT = 56 * 1024 * 1024

F32 = jnp.float32
BF16 = jnp.bfloat16


def _dot(a, b):
    return jnp.dot(a, b, preferred_element_type=F32)


def _layer_norm(v, g, b):
    mu = jnp.mean(v, axis=-1, keepdims=True)
    d = v - mu
    var = jnp.mean(d * d, axis=-1, keepdims=True)
    return d * lax.rsqrt(var + LN_EPS) * g + b


def _const_spec(shape):
    zeros = (0,) * len(shape)
    return pl.BlockSpec(shape, lambda *_: zeros, pipeline_mode=pl.Buffered(1))


def _params(n_axes, vmem=None):
    return pltpu.CompilerParams(dimension_semantics=("arbitrary",) * n_axes, vmem_limit_bytes=vmem)


def _ada_kernel(c_ref, w_ref, b_ref, o_ref):
    a = jax.nn.silu(c_ref[...]).astype(BF16)
    o_ref[0] = _dot(a, w_ref[0].astype(BF16)) + b_ref[0]


def _ada_modulation(c, ada_w, ada_b):
    b_, d = c.shape
    n_pairs = ada_w.shape[0] * ada_w.shape[1]
    w = ada_w.reshape(n_pairs, d, 3 * d)
    bias = ada_b.reshape(n_pairs, 1, 3 * d)
    mod = pl.pallas_call(
        _ada_kernel,
        grid=(n_pairs, 3),
        in_specs=[
            pl.BlockSpec((b_, d), lambda i, j: (0, 0)),
            pl.BlockSpec((1, d, d), lambda i, j: (i, 0, j)),
            pl.BlockSpec((1, 1, d), lambda i, j: (i, 0, j)),
        ],
        out_specs=pl.BlockSpec((1, b_, d), lambda i, j: (i, 0, j)),
        out_shape=jax.ShapeDtypeStruct((n_pairs, b_, 3 * d), F32),
        compiler_params=_params(2),
        name="ada",
    )(c, w, bias)
    return mod.reshape(n_pairs * b_ * 3, 1, d)


def _mod_spec(pair, which, b_, d):
    return pl.BlockSpec((1, 1, d), lambda b, i: ((pair * b_ + b) * 3 + which, 0, 0))


def _segment_sum(x, seg):
    hi = x.astype(BF16)
    lo = (x - hi.astype(F32)).astype(BF16)
    return _dot(hi, seg) + _dot(lo, seg)


def _inproj0_kernel(x_ref, shift_ref, scale_ref, wq_ref, wkv_ref, wu_ref, wv_ref, seg_ref, ng_ref,
                    wsp_ref, bsp_ref,
                    q_ref, kc_ref, vc_ref, ks_ref, vs_ref, kw_ref, vw_ref, gate_ref, og_ref):
    tm = x_ref.shape[1] // 2
    left = lax.broadcasted_iota(jnp.int32, (GMLP_CHUNK, LANES), 1) < HALF
    t_idx = lax.broadcasted_iota(jnp.int32, (GMLP_CHUNK, 2 * GMLP_CHUNK), 0)
    s_idx = lax.broadcasted_iota(jnp.int32, (GMLP_CHUNK, 2 * GMLP_CHUNK), 1) & (GMLP_CHUNK - 1)
    causal = s_idx <= t_idx

    def sub_tile(r0):
        rs = slice(r0, r0 + tm)
        h = (x_ref[0, rs, :] * (1.0 + scale_ref[0]) + shift_ref[0]).astype(BF16)
        q_ref[0, rs, :] = _dot(h, wq_ref[...]) * (HEAD_DIM ** -0.5 * LOG2_E)
        yield
        kv = _dot(h, wkv_ref[...])
        for n, ref in enumerate((kc_ref, vc_ref, ks_ref, vs_ref, kw_ref, vw_ref)):
            ref[0, rs, :] = kv[:, n * LANES:(n + 1) * LANES].astype(ref.dtype)
        gate_ref[0, rs, :] = jax.nn.sigmoid(kv[:, 6 * LANES:7 * LANES])
        yield
        u = jax.nn.gelu(_dot(h, wu_ref[...]))
        yield
        vf = jax.nn.gelu(_dot(h, wv_ref[...]))
        yield
        seg = seg_ref[...]
        inv = 1.0 / GMLP_HEAD_DIM
        mu = _segment_sum(vf, seg) * inv
        dv = vf - mu
        yield
        var = _segment_sum(dv * dv, seg) * inv
        vn = dv * lax.rsqrt(var + LN_EPS) * ng_ref[...]
        yield
        for p in range(GMLP_GROUPS // 2):
            cols = slice(p * LANES, (p + 1) * LANES)
            wcat = jnp.where(causal, wsp_ref[p], 0.0).astype(BF16)
            bias = bsp_ref[:, cols]
            for c in range(tm // GMLP_CHUNK):
                rows = slice(c * GMLP_CHUNK, (c + 1) * GMLP_CHUNK)
                vp = vn[rows, cols]
                v2 = jnp.concatenate([jnp.where(left, vp, 0.0), jnp.where(left, 0.0, vp)], axis=0).astype(BF16)
                mixed = _dot(wcat, v2)
                og_ref[0, r0 + c * GMLP_CHUNK:r0 + (c + 1) * GMLP_CHUNK, cols] = (
                    u[rows, cols] * (mixed + bias)).astype(og_ref.dtype)
            yield

    a, b = sub_tile(0), sub_tile(tm)
    n_proj = 4
    for _ in range(n_proj):
        next(a)
    for _ in range(n_proj):
        next(b)
        next(a, None)
    for _ in a:
        pass
    for _ in b:
        pass


def _inproj0(x, mod, w_in, norm_g, w_s, b_s):
    b_, s_, d = x.shape
    tm = 2 * ROW_TILE
    qd = NSA_Q_HEADS * HEAD_DIM
    kvd = NSA_KV_HEADS * HEAD_DIM
    gd = NSA_Q_HEADS * N_BRANCHES
    gm = GMLP_GROUPS * GMLP_HEAD_DIM
    o_kv = qd
    o_g = qd + 6 * kvd
    o_u = o_g + gd
    o_v = o_u + gm
    wq = w_in[:, :qd].astype(BF16)
    wkv = jnp.concatenate([w_in[:, o_kv:o_g], w_in[:, o_g:o_u],
                           jnp.zeros((d, LANES - gd), w_in.dtype)], axis=1).astype(BF16)
    wu = w_in[:, o_u:o_v].astype(BF16)
    wv = w_in[:, o_v:o_v + gm].astype(BF16)
    seg = jnp.asarray(np.kron(np.eye(GMLP_GROUPS), np.ones((GMLP_HEAD_DIM, GMLP_HEAD_DIM))), BF16)
    ng = norm_g.reshape(1, gm)
    wsp = w_s.reshape(GMLP_GROUPS // 2, 2, GMLP_CHUNK, GMLP_CHUNK).transpose(0, 2, 1, 3).reshape(
        GMLP_GROUPS // 2, GMLP_CHUNK, 2 * GMLP_CHUNK)
    bsp = jnp.broadcast_to(b_s.T[:, :, None], (GMLP_CHUNK, GMLP_GROUPS, GMLP_HEAD_DIM)).reshape(GMLP_CHUNK, gm)

    tok = lambda w: pl.BlockSpec((1, tm, w), lambda b, i: (b, i, 0))
    shp = lambda w, dt: jax.ShapeDtypeStruct((b_, s_, w), dt)
    return pl.pallas_call(
        _inproj0_kernel,
        grid=(b_, s_ // tm),
        in_specs=[tok(d), _mod_spec(0, 0, b_, d), _mod_spec(0, 1, b_, d),
                  _const_spec(wq.shape), _const_spec(wkv.shape), _const_spec(wu.shape), _const_spec(wv.shape),
                  _const_spec(seg.shape), _const_spec(ng.shape), _const_spec(wsp.shape), _const_spec(bsp.shape)],
        out_specs=[tok(qd)] + [tok(LANES)] * 7 + [tok(gm)],
        out_shape=[shp(qd, F32), shp(LANES, F32), shp(LANES, F32)] + [shp(LANES, BF16)] * 4
                  + [shp(LANES, F32), shp(gm, BF16)],
        compiler_params=_params(2, VMEM_LIMIT),
        name="inproj0",
    )(x, mod, mod, wq, wkv, wu, wv, seg, ng, wsp, bsp)


def _compress_kernel(kch_ref, vch_ref, pos_ref, w1a_ref, w1b_ref, w2_ref, kco_ref, vco_ref):
    for t, (src, dst) in enumerate(((kch_ref, kco_ref), (vch_ref, vco_ref))):
        ch = src[0]
        n = ch.shape[0]
        first = (ch + pos_ref[t, 0:1]).astype(BF16)
        second = (ch + pos_ref[t, 1:2]).astype(BF16)
        ha = _dot(first, w1a_ref[t])
        hb = _dot(second, w1b_ref[t])
        h1 = ha + pltpu.roll(hb, n - 1, axis=0)
        dst[0] = _dot(jax.nn.gelu(h1).astype(BF16), w2_ref[t]).astype(dst.dtype)


def _compress(kc, vc, cmp_pos, cmp_w1, cmp_w2):
    b_, s_, _ = kc.shape
    nch = s_ // CMP_STRIDE
    r = CMP_LEN // CMP_STRIDE
    assert r == 2
    eye = jnp.eye(NSA_KV_HEADS, dtype=cmp_w1.dtype)
    w1 = cmp_w1.reshape(2, r, CMP_STRIDE, HEAD_DIM, CMP_HIDDEN)
    w1 = jnp.einsum("thldc,gk->thlgdkc", w1, eye).reshape(
        2, r, CMP_STRIDE * LANES, NSA_KV_HEADS * CMP_HIDDEN).astype(BF16)
    w2 = jnp.einsum("tcd,gk->tgckd", cmp_w2, eye).reshape(2, NSA_KV_HEADS * CMP_HIDDEN, LANES).astype(BF16)
    pos = jnp.broadcast_to(cmp_pos.reshape(2, r, CMP_STRIDE, 1, HEAD_DIM),
                           (2, r, CMP_STRIDE, NSA_KV_HEADS, HEAD_DIM)).reshape(2, r, CMP_STRIDE * LANES)
    kch = kc.reshape(b_, nch, CMP_STRIDE * LANES)
    vch = vc.reshape(b_, nch, CMP_STRIDE * LANES)
    ch_spec = pl.BlockSpec((1, nch, CMP_STRIDE * LANES), lambda b: (b, 0, 0))
    out_spec = pl.BlockSpec((1, nch, LANES), lambda b: (b, 0, 0))
    out_shape = jax.ShapeDtypeStruct((b_, nch, LANES), BF16)
    return pl.pallas_call(
        _compress_kernel,
        grid=(b_,),
        in_specs=[ch_spec, ch_spec, _const_spec(pos.shape), _const_spec(w1[:, 0].shape),
                  _const_spec(w1[:, 1].shape), _const_spec(w2.shape)],
        out_specs=[out_spec, out_spec],
        out_shape=[out_shape, out_shape],
        compiler_params=_params(1, VMEM_LIMIT),
        name="compress",
    )(kch, vch, pos, w1[:, 0], w1[:, 1], w2)


def _nsa_kernel(q_ref, gate_ref, kc_ref, vc_ref, ks_ref, vs_ref, kw_ref, vw_ref, cover_ref, o_ref,
                kaug_ref, vst_ref, vwt_ref, vct_ref, s0_ref, s1_ref, c0_ref, c1_ref, w0_ref, w1_ref):
    i = pl.program_id(1)
    start = i * Q_TILE
    s_len = ks_ref.shape[1]
    n_sel = s_len // SEL_LEN
    n_cmp = kc_ref.shape[1]
    rows = NSA_GROUP * Q_TILE
    sub = KV_CHUNK // LANES

    @pl.when(i == 0)
    def _():
        def build(c, carry):
            off = pl.multiple_of(c * KV_CHUNK, KV_CHUNK)
            key_blk = (off + lax.broadcasted_iota(jnp.int32, (KV_CHUNK, LANES), 0)) >> SEL_SHIFT
            ln = lax.broadcasted_iota(jnp.int32, (KV_CHUNK, LANES), 1)
            ks = ks_ref[0, pl.ds(off, KV_CHUNK), :].astype(F32)
            for g in range(NSA_KV_HEADS):
                onehot = (ln - (1 - g) * HALF == key_blk).astype(F32)
                kaug_ref[g, pl.ds(off, KV_CHUNK), :] = jnp.where((ln >> HALF_SHIFT) == g, ks, onehot).astype(BF16)
            for j in range(sub):
                blk = pl.ds(pl.multiple_of(off + j * LANES, LANES), LANES)
                cols = slice(j * LANES, (j + 1) * LANES)
                vs_t = vs_ref[0, blk, :].astype(F32).T.astype(BF16)
                vw_t = vw_ref[0, blk, :].astype(F32).T.astype(BF16)
                for g in range(NSA_KV_HEADS):
                    vst_ref[c, g, 0:HEAD_DIM, cols] = vs_t[g * HALF:(g + 1) * HALF]
                    vwt_ref[c * sub + j, g, 0:HEAD_DIM, :] = vw_t[g * HALF:(g + 1) * HALF]
            return carry
        lax.fori_loop(0, s_len // KV_CHUNK, build, 0)
        for g in range(NSA_KV_HEADS):
            for j in range(n_cmp // LANES):
                cols = slice(j * LANES, (j + 1) * LANES)
                vct_ref[g, 0:HEAD_DIM, cols] = vc_ref[0, cols, :].astype(F32).T[g * HALF:(g + 1) * HALF].astype(BF16)
        ones = lambda ref: jnp.ones(ref.shape[:-2] + (V_ROWS - HEAD_DIM, ref.shape[-1]), BF16)
        vst_ref[:, :, HEAD_DIM:V_ROWS, :] = ones(vst_ref)
        vwt_ref[:, :, HEAD_DIM:V_ROWS, :] = ones(vwt_ref)
        vct_ref[:, HEAD_DIM:V_ROWS, :] = ones(vct_ref)

    q = q_ref[0]
    q_t = jnp.concatenate([q[:, c * LANES:(c + 1) * LANES].T for c in range(q.shape[1] // LANES)], axis=0)
    g_t = gate_ref[0].T
    tpos = start + lax.broadcasted_iota(jnp.int32, (1, Q_TILE), 1)
    j_idx = lax.broadcasted_iota(jnp.int32, (n_sel, 1), 0)
    zeros_half = jnp.zeros((HALF, rows), F32)
    halves = lambda g, own, other: jnp.concatenate([own, other] if g == 0 else [other, own], axis=0).astype(BF16)
    per_head = lambda a: jnp.concatenate([a] * NSA_GROUP, axis=1)
    key_iota = lambda n: lax.broadcasted_iota(jnp.int32, (n, 1), 0)
    strip_max = lambda s: jnp.max(s.reshape(s.shape[0] // 8, 8, rows), axis=0)
    col_max = lambda m8: jnp.max(m8, axis=0, keepdims=True)

    qg = [jnp.concatenate([q_t[(NSA_GROUP * g + r) * HEAD_DIM:(NSA_GROUP * g + r + 1) * HEAD_DIM, :]
                           for r in range(NSA_GROUP)], axis=1) for g in range(NSA_KV_HEADS)]
    q0 = [halves(g, qg[g], zeros_half) for g in range(NSA_KV_HEADS)]
    cbuf, wbuf = (c0_ref, c1_ref), (w0_ref, w1_ref)

    cmp_bias = per_head(jnp.where(key_iota(n_cmp) * CMP_STRIDE + (CMP_LEN - 1) <= tpos, 0.0, NEG_INF))
    has_cmp_key = per_head(tpos >= CMP_LEN - 1)

    def cmp_qk(g):
        s = _dot(kc_ref[0], q0[g]) + cmp_bias
        cbuf[g][...] = s
        return strip_max(s)

    def cmp_pv(g, m8):
        e = jnp.exp2(cbuf[g][...] - col_max(m8))
        nd = _dot(vct_ref[g], e.astype(BF16))
        inv = jnp.where(has_cmp_key, 1.0 / nd[HEAD_DIM:HEAD_DIM + 1], 0.0)
        p = e * inv
        p_sum = p[:, 0:Q_TILE]
        for r in range(1, NSA_GROUP):
            p_sum = p_sum + p[:, r * Q_TILE:(r + 1) * Q_TILE]
        imp = jnp.dot(cover_ref[...], p_sum, precision=lax.Precision.HIGHEST,
                      preferred_element_type=F32)
        return nd[0:HEAD_DIM] * inv, imp

    def select(g, imp):
        cur = tpos >> SEL_SHIFT
        forced = (j_idx == 0) | (j_idx == cur) | (j_idx == cur - 1)
        score = jnp.where(forced, FORCE_SCORE, jnp.where(j_idx * SEL_LEN <= tpos, imp, -FORCE_SCORE))
        strips = [score[b * 8:(b + 1) * 8] for b in range(n_sel // 8)]
        ranks = [jnp.zeros((8, Q_TILE), jnp.int32) for _ in strips]
        for jp in range(n_sel):
            other = score[jp:jp + 1, :]
            for b, strip in enumerate(strips):
                if jp < b * 8:
                    beats = other >= strip
                elif jp >= (b + 1) * 8:
                    beats = other > strip
                else:
                    beats = (other > strip) | ((other == strip) & (jp < j_idx[b * 8:(b + 1) * 8]))
                ranks[b] = ranks[b] + beats.astype(jnp.int32)
        rank = jnp.concatenate(ranks, axis=0)
        bias = jnp.where(rank < min(SEL_TOP_N, n_sel), 0.0, NEG_INF)
        if n_sel < HALF:
            bias = jnp.concatenate([bias, jnp.zeros((HALF - n_sel, Q_TILE), F32)], axis=0)
        return halves(g, qg[g], per_head(bias))

    n_blk = (WINDOW + Q_TILE) // LANES
    wb = jnp.maximum(i * (Q_TILE // LANES) - WINDOW // LANES, 0)
    ws = pl.multiple_of(wb * LANES, LANES)
    dist = tpos - (ws + key_iota(n_blk * LANES))
    win_bias = per_head(jnp.where((dist >= 0) & (dist < WINDOW), 0.0, NEG_INF))

    def win_qk(g):
        s = _dot(kw_ref[0, pl.ds(ws, n_blk * LANES), :], q0[g]) + win_bias
        wbuf[g][...] = s
        return strip_max(s)

    def win_pv(g, m8):
        m = col_max(m8)
        nd = jnp.zeros((V_ROWS, rows), F32)
        for j in range(n_blk):
            blk = slice(j * LANES, (j + 1) * LANES)
            nd = nd + _dot(vwt_ref[wb + j, g], jnp.exp2(wbuf[g][blk, :] - m).astype(BF16))
        return nd[0:HEAD_DIM] * (1.0 / nd[HEAD_DIM:HEAD_DIM + 1])

    def mix(g, o_cmp, o_win):
        per_r = []
        for r in range(NSA_GROUP):
            sl = slice(r * Q_TILE, (r + 1) * Q_TILE)
            c0 = (NSA_GROUP * g + r) * N_BRANCHES
            per_r.append(g_t[c0:c0 + 1] * o_cmp[:, sl] + g_t[c0 + 2:c0 + 3] * o_win[:, sl])
        return per_r

    cm0 = cmp_qk(0)
    cm1 = cmp_qk(1)
    o_cmp0, imp0 = cmp_pv(0, cm0)
    wm0 = win_qk(0)
    o_cmp1, imp1 = cmp_pv(1, cm1)
    qa = [select(0, imp0)]
    wm1 = win_qk(1)
    mixed = [mix(0, o_cmp0, win_pv(0, wm0))]
    qa.append(select(1, imp1))
    mixed.append(mix(1, o_cmp1, win_pv(1, wm1)))

    diag = lax.div(i, KV_CHUNK // Q_TILE)

    def scores(g, c):
        off = pl.multiple_of(c * KV_CHUNK, KV_CHUNK)
        return _dot(kaug_ref[g, pl.ds(off, KV_CHUNK), :], qa[g])

    sbuf = (s0_ref, s1_ref)
    causal_bias = per_head(jnp.where(diag * KV_CHUNK + key_iota(KV_CHUNK) <= tpos, 0.0, NEG_INF))

    def qk_stage(g, c, masked=False):
        s = scores(g, c)
        if masked:
            s = s + causal_bias
        sbuf[g][...] = s
        return strip_max(s)

    def pv_stage(g, c, m8, state):
        m, acc = state
        m_new = jnp.maximum(m, col_max(m8))
        p = jnp.exp2(sbuf[g][...] - m_new).astype(BF16)
        return m_new, jnp.exp2(m - m_new) * acc + _dot(vst_ref[c, g], p)

    def sel_chunk(c, carry):
        m8_0, st0, st1 = carry
        m8_1 = qk_stage(1, c)
        st0 = pv_stage(0, c, m8_0, st0)
        m8_0 = qk_stage(0, c + 1)
        st1 = pv_stage(1, c, m8_1, st1)
        return m8_0, st0, st1

    init = (jnp.full((1, rows), NEG_INF, F32), jnp.zeros((V_ROWS, rows), F32))
    _, st0, st1 = lax.fori_loop(0, diag, sel_chunk, (qk_stage(0, 0), init, init))
    s_d0 = s0_ref[...] + causal_bias
    s0_ref[...] = s_d0
    m8_1 = qk_stage(1, diag, masked=True)
    acc_sel = (pv_stage(0, diag, strip_max(s_d0), st0)[1], pv_stage(1, diag, m8_1, st1)[1])

    heads_out = []
    for g in range(NSA_KV_HEADS):
        acc = acc_sel[g]
        o_sel = acc[0:HEAD_DIM] * (1.0 / acc[HEAD_DIM:HEAD_DIM + 1])
        for r in range(NSA_GROUP):
            sl = slice(r * Q_TILE, (r + 1) * Q_TILE)
            gate = g_t[(NSA_GROUP * g + r) * N_BRANCHES + 1:(NSA_GROUP * g + r) * N_BRANCHES + 2]
            heads_out.append(mixed[g][r] + gate * o_sel[:, sl])
    for c in range(len(heads_out) // 2):
        pair = jnp.concatenate(heads_out[2 * c:2 * c + 2], axis=0)
        o_ref[0, :, c * LANES:(c + 1) * LANES] = pair.T.astype(o_ref.dtype)


def _cover_matrix_t(n_sel, n_cmp_padded):
    jj = np.arange(n_sel)[:, None]
    kk = np.arange(n_cmp_padded)[None, :]
    cover = ((kk * CMP_STRIDE < (jj + 1) * SEL_LEN) & (kk * CMP_STRIDE + CMP_LEN > jj * SEL_LEN)
             & (kk < n_cmp_padded - 1))
    return jnp.asarray(cover, F32)


def _nsa(q, gates, kcc, vcc, ks, vs, kw, vw):
    b_, s_, qd = q.shape
    n_cmp = kcc.shape[1]
    n_sel = s_ // SEL_LEN
    assert Q_TILE % LANES == 0 and KV_CHUNK % Q_TILE == 0 and n_sel <= HALF and n_sel % 8 == 0
    assert n_cmp % LANES == 0 and s_ % KV_CHUNK == 0 and s_ >= WINDOW + Q_TILE and WINDOW % LANES == 0
    cover = _cover_matrix_t(n_sel, n_cmp)
    tok = lambda w: pl.BlockSpec((1, Q_TILE, w), lambda b, i: (b, i, 0))
    seq = lambda n: pl.BlockSpec((1, n, LANES), lambda b, i: (b, 0, 0))
    return pl.pallas_call(
        _nsa_kernel,
        grid=(b_, s_ // Q_TILE),
        in_specs=[tok(qd), tok(LANES), seq(n_cmp), seq(n_cmp), seq(s_), seq(s_), seq(s_), seq(s_),
                  _const_spec(cover.shape)],
        out_specs=tok(qd),
        out_shape=jax.ShapeDtypeStruct((b_, s_, qd), BF16),
        scratch_shapes=[pltpu.VMEM((NSA_KV_HEADS, s_, LANES), BF16),
                        pltpu.VMEM((s_ // KV_CHUNK, NSA_KV_HEADS, V_ROWS, KV_CHUNK), BF16),
                        pltpu.VMEM((s_ // LANES, NSA_KV_HEADS, V_ROWS, LANES), BF16),
                        pltpu.VMEM((NSA_KV_HEADS, V_ROWS, n_cmp), BF16),
                        pltpu.VMEM((KV_CHUNK, NSA_GROUP * Q_TILE), F32),
                        pltpu.VMEM((KV_CHUNK, NSA_GROUP * Q_TILE), F32),
                        pltpu.VMEM((n_cmp, NSA_GROUP * Q_TILE), F32),
                        pltpu.VMEM((n_cmp, NSA_GROUP * Q_TILE), F32),
                        pltpu.VMEM((WINDOW + Q_TILE, NSA_GROUP * Q_TILE), F32),
                        pltpu.VMEM((WINDOW + Q_TILE, NSA_GROUP * Q_TILE), F32)],
        compiler_params=_params(2, VMEM_LIMIT),
        name="nsa",
    )(q, gates, kcc, vcc, ks, vs, kw, vw, cover)


def _swiglu_steps(y, shift, scale, gate, w_in_ref, w_out_ref):
    h = (y * (1.0 + scale) + shift).astype(BF16)
    hid = w_out_ref.shape[0]
    acc = jnp.zeros(y.shape, F32)
    for c in range(hid // FFN_CHUNK):
        lo = c * FFN_CHUNK
        g_ = _dot(h, w_in_ref[:, lo:lo + FFN_CHUNK])
        u_ = _dot(h, w_in_ref[:, hid + lo:hid + lo + FFN_CHUNK])
        acc = acc + _dot((jax.nn.silu(g_) * u_).astype(BF16), w_out_ref[lo:lo + FFN_CHUNK, :])
        yield None
    yield ALPHA * y + (1.0 + gate) * acc


def _mix0_kernel(x_ref, on_ref, og_ref, gate1_ref, wo_ref, ln1g_ref, ln1b_ref,
                 shift2_ref, scale2_ref, gate2_ref, fin_ref, fout_ref, ln2g_ref, ln2b_ref, o_ref):
    half = on_ref.shape[2]
    tm = x_ref.shape[1] // 2
    ra, rb = slice(0, tm), slice(tm, 2 * tm)
    w_out = lambda r: _dot(on_ref[0, r, :], wo_ref[0:half, :]) + _dot(og_ref[0, r, :], wo_ref[half:2 * half, :])
    ln1 = lambda r, out: _layer_norm(ALPHA * x_ref[0, r, :] + (1.0 + gate1_ref[0]) * out, ln1g_ref[...], ln1b_ref[...])
    ffn = lambda y: _swiglu_steps(y, shift2_ref[0], scale2_ref[0], gate2_ref[0], fin_ref, fout_ref)
    out_a = w_out(ra)
    out_b = w_out(rb)
    steps_a = ffn(ln1(ra, out_a))
    next(steps_a)
    steps_b = ffn(ln1(rb, out_b))
    res_a = [v for v in steps_a if v is not None][0]
    next(steps_b)
    o_ref[0, ra, :] = _layer_norm(res_a, ln2g_ref[...], ln2b_ref[...])
    res_b = [v for v in steps_b if v is not None][0]
    o_ref[0, rb, :] = _layer_norm(res_b, ln2g_ref[...], ln2b_ref[...])


def _mix0(x, o_nsa, o_gmlp, mod, w_out, ln_g, ln_b, ffn_w_in, ffn_w_out):
    b_, s_, d = x.shape
    tm = 2 * ROW_TILE
    wo = w_out.astype(BF16)
    fin = ffn_w_in.astype(BF16)
    fout = ffn_w_out.astype(BF16)
    row = lambda v: v.reshape(1, d)
    tok = lambda w: pl.BlockSpec((1, tm, w), lambda b, i: (b, i, 0))
    return pl.pallas_call(
        _mix0_kernel,
        grid=(b_, s_ // tm),
        in_specs=[tok(d), tok(o_nsa.shape[2]), tok(o_gmlp.shape[2]), _mod_spec(0, 2, b_, d),
                  _const_spec(wo.shape), _const_spec((1, d)), _const_spec((1, d)),
                  _mod_spec(1, 0, b_, d), _mod_spec(1, 1, b_, d), _mod_spec(1, 2, b_, d),
                  _const_spec(fin.shape), _const_spec(fout.shape), _const_spec((1, d)), _const_spec((1, d))],
        out_specs=tok(d),
        out_shape=jax.ShapeDtypeStruct(x.shape, F32),
        compiler_params=_params(2, VMEM_LIMIT),
        name="mix0",
    )(x, o_nsa, o_gmlp, mod, wo, row(ln_g[0]), row(ln_b[0]), mod, mod, mod, fin, fout, row(ln_g[1]), row(ln_b[1]))


def _layer1_kernel(x_ref, shift1_ref, scale1_ref, gate1_ref, win_ref, cw_ref, wo_ref, ln1g_ref, ln1b_ref,
                   shift2_ref, scale2_ref, gate2_ref, fin_ref, fout_ref, ln2g_ref, ln2b_ref, o_ref, carry_ref):
    i = pl.program_id(1)
    tm, d = x_ref.shape[1] // 2, x_ref.shape[2]
    halo = carry_ref.shape[0]
    ra, rb = slice(0, tm), slice(tm, 2 * tm)

    @pl.when(i == 0)
    def _():
        carry_ref[...] = jnp.zeros(carry_ref.shape, F32)

    def project(r):
        h = (x_ref[0, r, :] * (1.0 + scale1_ref[0]) + shift1_ref[0]).astype(BF16)
        return _dot(h, win_ref[:, 0:d]), _dot(h, win_ref[:, d:2 * d]) * _dot(h, win_ref[:, 2 * d:3 * d])

    def conv_out(b_gate, cz, prev):
        row = lax.broadcasted_iota(jnp.int32, (tm, 1), 0)
        back1 = jnp.where(row == 0, prev[halo - 1:halo], pltpu.roll(cz, 1, axis=0))
        back2 = jnp.where(row == 0, prev[halo - 2:halo - 1],
                          jnp.where(row == 1, prev[halo - 1:halo], pltpu.roll(cz, 2, axis=0)))
        cw = cw_ref[...]
        y = cw[0:1] * back2 + cw[1:2] * back1 + cw[2:3] * cz
        return _dot((b_gate * y).astype(BF16), wo_ref[...])

    ln1 = lambda r, out: _layer_norm(ALPHA * x_ref[0, r, :] + (1.0 + gate1_ref[0]) * out,
                                     ln1g_ref[...], ln1b_ref[...])
    ffn = lambda y: _swiglu_steps(y, shift2_ref[0], scale2_ref[0], gate2_ref[0], fin_ref, fout_ref)
    bg_a, cz_a = project(ra)
    bg_b, cz_b = project(rb)
    out_a = conv_out(bg_a, cz_a, carry_ref[...])
    out_b = conv_out(bg_b, cz_b, cz_a[tm - halo:tm])
    carry_ref[...] = cz_b[tm - halo:tm]
    steps_a = ffn(ln1(ra, out_a))
    next(steps_a)
    steps_b = ffn(ln1(rb, out_b))
    res_a = [v for v in steps_a if v is not None][0]
    next(steps_b)
    o_ref[0, ra, :] = _layer_norm(res_a, ln2g_ref[...], ln2b_ref[...])
    res_b = [v for v in steps_b if v is not None][0]
    o_ref[0, rb, :] = _layer_norm(res_b, ln2g_ref[...], ln2b_ref[...])


def _layer1(x, mod, w_in, conv_w, w_out, ln_g, ln_b, ffn_w_in, ffn_w_out):
    b_, s_, d = x.shape
    tm = 2 * ROW_TILE
    assert conv_w.shape[0] == CONV_WIDTH == 3
    win = w_in.astype(BF16)
    wo = w_out.astype(BF16)
    fin = ffn_w_in.astype(BF16)
    fout = ffn_w_out.astype(BF16)
    row = lambda v: v.reshape(1, d)
    tok = pl.BlockSpec((1, tm, d), lambda b, i: (b, i, 0))
    return pl.pallas_call(
        _layer1_kernel,
        grid=(b_, s_ // tm),
        in_specs=[tok, _mod_spec(2, 0, b_, d), _mod_spec(2, 1, b_, d), _mod_spec(2, 2, b_, d),
                  _const_spec(win.shape), _const_spec(conv_w.shape), _const_spec(wo.shape),
                  _const_spec((1, d)), _const_spec((1, d)),
                  _mod_spec(3, 0, b_, d), _mod_spec(3, 1, b_, d), _mod_spec(3, 2, b_, d),
                  _const_spec(fin.shape), _const_spec(fout.shape), _const_spec((1, d)), _const_spec((1, d))],
        out_specs=tok,
        out_shape=jax.ShapeDtypeStruct(x.shape, F32),
        scratch_shapes=[pltpu.VMEM((8, d), F32)],
        compiler_params=_params(2, VMEM_LIMIT),
        name="layer1",
    )(x, mod, mod, mod, win, conv_w, wo, row(ln_g[0]), row(ln_b[0]), mod, mod, mod, fin, fout,
      row(ln_g[1]), row(ln_b[1]))


def kernel(x, c, ada_w, ada_b, ln_g, ln_b, even_w_in, even_cmp_pos, even_cmp_w1, even_cmp_w2, even_gmlp_norm_g,
           even_gmlp_ws, even_gmlp_bs, even_w_out, odd_w_in, odd_conv_w, odd_w_out, ffn_w_in, ffn_w_out):
    assert ada_w.shape[0] == DEPTH == 2
    mod = _ada_modulation(c, ada_w, ada_b)
    q, kc, vc, ks, vs, kw, vw, gates, o_gmlp = _inproj0(
        x, mod, even_w_in[0], even_gmlp_norm_g[0], even_gmlp_ws[0], even_gmlp_bs[0])
    kcc, vcc = _compress(kc, vc, even_cmp_pos[0], even_cmp_w1[0], even_cmp_w2[0])
    o_nsa = _nsa(q, gates, kcc, vcc, ks, vs, kw, vw)
    x = _mix0(x, o_nsa, o_gmlp, mod, even_w_out[0], ln_g[0], ln_b[0], ffn_w_in[0], ffn_w_out[0])
    return _layer1(x, mod, odd_w_in[0], odd_conv_w[0], odd_w_out[0], ln_g[1], ln_b[1], ffn_w_in[1], ffn_w_out[1])
```

```python
import numpy as np
import jax
import jax.numpy as jnp
from jax import lax
from jax.experimental import pallas as pl
from jax.experimental.pallas import tpu as pltpu

HEAD_DIM = 64
NSA_Q_HEADS = 8
NSA_KV_HEADS = 2
NSA_GROUP = NSA_Q_HEADS // NSA_KV_HEADS
CMP_LEN = 32
CMP_STRIDE = 16
CMP_HIDDEN = 128
SEL_LEN = 64
SEL_TOP_N = 16
WINDOW = 512
N_BRANCHES = 3
GMLP_GROUPS = 8
GMLP_HEAD_DIM = 64
GMLP_CHUNK = 128
CONV_WIDTH = 3
DEPTH = 2
ALPHA = (2 * DEPTH) ** 0.25
LN_EPS = 1e-5
NEG_INF = -1e30
FORCE_SCORE = 1e4

LANES = 128
HALF = LANES // 2
HALF_SHIFT = 6
SEL_SHIFT = 6
Q_TILE = 256
KV_CHUNK = 512
V_ROWS = HEAD_DIM + 16
LOG2_E = 1.4426950408889634
ROW_TILE = 512
FFN_CHUNK = 256
VMEM_LIMIT = 56 * 1024 * 1024

F32 = jnp.float32
BF16 = jnp.bfloat16


def _dot(a, b):
    return jnp.dot(a, b, preferred_element_type=F32)


def _layer_norm(v, g, b):
    mu = jnp.mean(v, axis=-1, keepdims=True)
    d = v - mu
    var = jnp.mean(d * d, axis=-1, keepdims=True)
    return d * lax.rsqrt(var + LN_EPS) * g + b


def _const_spec(shape):
    zeros = (0,) * len(shape)
    return pl.BlockSpec(shape, lambda *_: zeros, pipeline_mode=pl.Buffered(1))


def _params(n_axes, vmem=None):
    return pltpu.CompilerParams(dimension_semantics=("arbitrary",) * n_axes, vmem_limit_bytes=vmem)


def _ada_kernel(c_ref, w_ref, b_ref, o_ref):
    a = jax.nn.silu(c_ref[...]).astype(BF16)
    o_ref[0] = _dot(a, w_ref[0].astype(BF16)) + b_ref[0]


def _ada_modulation(c, ada_w, ada_b):
    b_, d = c.shape
    n_pairs = ada_w.shape[0] * ada_w.shape[1]
    w = ada_w.reshape(n_pairs, d, 3 * d)
    bias = ada_b.reshape(n_pairs, 1, 3 * d)
    mod = pl.pallas_call(
        _ada_kernel,
        grid=(n_pairs, 3),
        in_specs=[
            pl.BlockSpec((b_, d), lambda i, j: (0, 0)),
            pl.BlockSpec((1, d, d), lambda i, j: (i, 0, j)),
            pl.BlockSpec((1, 1, d), lambda i, j: (i, 0, j)),
        ],
        out_specs=pl.BlockSpec((1, b_, d), lambda i, j: (i, 0, j)),
        out_shape=jax.ShapeDtypeStruct((n_pairs, b_, 3 * d), F32),
        compiler_params=_params(2),
        name="ada",
    )(c, w, bias)
    return mod.reshape(n_pairs * b_ * 3, 1, d)


def _mod_spec(pair, which, b_, d):
    return pl.BlockSpec((1, 1, d), lambda b, i: ((pair * b_ + b) * 3 + which, 0, 0))


def _segment_sum(x, seg):
    hi = x.astype(BF16)
    lo = (x - hi.astype(F32)).astype(BF16)
    return _dot(hi, seg) + _dot(lo, seg)


def _inproj0_kernel(x_ref, shift_ref, scale_ref, wq_ref, wkv_ref, wu_ref, wv_ref, seg_ref, ng_ref,
                    wsp_ref, bsp_ref,
                    q_ref, kc_ref, vc_ref, ks_ref, vs_ref, kw_ref, vw_ref, gate_ref, og_ref):
    tm = x_ref.shape[1] // 2
    left = lax.broadcasted_iota(jnp.int32, (GMLP_CHUNK, LANES), 1) < HALF
    t_idx = lax.broadcasted_iota(jnp.int32, (GMLP_CHUNK, 2 * GMLP_CHUNK), 0)
    s_idx = lax.broadcasted_iota(jnp.int32, (GMLP_CHUNK, 2 * GMLP_CHUNK), 1) & (GMLP_CHUNK - 1)
    causal = s_idx <= t_idx

    def sub_tile(r0):
        rs = slice(r0, r0 + tm)
        h = (x_ref[0, rs, :] * (1.0 + scale_ref[0]) + shift_ref[0]).astype(BF16)
        q_ref[0, rs, :] = _dot(h, wq_ref[...]) * (HEAD_DIM ** -0.5 * LOG2_E)
        yield
        kv = _dot(h, wkv_ref[...])
        for n, ref in enumerate((kc_ref, vc_ref, ks_ref, vs_ref, kw_ref, vw_ref)):
            ref[0, rs, :] = kv[:, n * LANES:(n + 1) * LANES].astype(ref.dtype)
        gate_ref[0, rs, :] = jax.nn.sigmoid(kv[:, 6 * LANES:7 * LANES])
        yield
        u = jax.nn.gelu(_dot(h, wu_ref[...]))
        yield
        vf = jax.nn.gelu(_dot(h, wv_ref[...]))
        yield
        seg = seg_ref[...]
        inv = 1.0 / GMLP_HEAD_DIM
        mu = _segment_sum(vf, seg) * inv
        dv = vf - mu
        yield
        var = _segment_sum(dv * dv, seg) * inv
        vn = dv * lax.rsqrt(var + LN_EPS) * ng_ref[...]
        yield
        for p in range(GMLP_GROUPS // 2):
            cols = slice(p * LANES, (p + 1) * LANES)
            wcat = jnp.where(causal, wsp_ref[p], 0.0).astype(BF16)
            bias = bsp_ref[:, cols]
            for c in range(tm // GMLP_CHUNK):
                rows = slice(c * GMLP_CHUNK, (c + 1) * GMLP_CHUNK)
                vp = vn[rows, cols]
                v2 = jnp.concatenate([jnp.where(left, vp, 0.0), jnp.where(left, 0.0, vp)], axis=0).astype(BF16)
                mixed = _dot(wcat, v2)
                og_ref[0, r0 + c * GMLP_CHUNK:r0 + (c + 1) * GMLP_CHUNK, cols] = (
                    u[rows, cols] * (mixed + bias)).astype(og_ref.dtype)
            yield

    a, b = sub_tile(0), sub_tile(tm)
    n_proj = 4
    for _ in range(n_proj):
        next(a)
    for _ in range(n_proj):
        next(b)
        next(a, None)
    for _ in a:
        pass
    for _ in b:
        pass


def _inproj0(x, mod, w_in, norm_g, w_s, b_s):
    b_, s_, d = x.shape
    tm = 2 * ROW_TILE
    qd = NSA_Q_HEADS * HEAD_DIM
    kvd = NSA_KV_HEADS * HEAD_DIM
    gd = NSA_Q_HEADS * N_BRANCHES
    gm = GMLP_GROUPS * GMLP_HEAD_DIM
    o_kv = qd
    o_g = qd + 6 * kvd
    o_u = o_g + gd
    o_v = o_u + gm
    wq = w_in[:, :qd].astype(BF16)
    wkv = jnp.concatenate([w_in[:, o_kv:o_g], w_in[:, o_g:o_u],
                           jnp.zeros((d, LANES - gd), w_in.dtype)], axis=1).astype(BF16)
    wu = w_in[:, o_u:o_v].astype(BF16)
    wv = w_in[:, o_v:o_v + gm].astype(BF16)
    seg = jnp.asarray(np.kron(np.eye(GMLP_GROUPS), np.ones((GMLP_HEAD_DIM, GMLP_HEAD_DIM))), BF16)
    ng = norm_g.reshape(1, gm)
    wsp = w_s.reshape(GMLP_GROUPS // 2, 2, GMLP_CHUNK, GMLP_CHUNK).transpose(0, 2, 1, 3).reshape(
        GMLP_GROUPS // 2, GMLP_CHUNK, 2 * GMLP_CHUNK)
    bsp = jnp.broadcast_to(b_s.T[:, :, None], (GMLP_CHUNK, GMLP_GROUPS, GMLP_HEAD_DIM)).reshape(GMLP_CHUNK, gm)

    tok = lambda w: pl.BlockSpec((1, tm, w), lambda b, i: (b, i, 0))
    shp = lambda w, dt: jax.ShapeDtypeStruct((b_, s_, w), dt)
    return pl.pallas_call(
        _inproj0_kernel,
        grid=(b_, s_ // tm),
        in_specs=[tok(d), _mod_spec(0, 0, b_, d), _mod_spec(0, 1, b_, d),
                  _const_spec(wq.shape), _const_spec(wkv.shape), _const_spec(wu.shape), _const_spec(wv.shape),
                  _const_spec(seg.shape), _const_spec(ng.shape), _const_spec(wsp.shape), _const_spec(bsp.shape)],
        out_specs=[tok(qd)] + [tok(LANES)] * 7 + [tok(gm)],
        out_shape=[shp(qd, F32), shp(LANES, F32), shp(LANES, F32)] + [shp(LANES, BF16)] * 4
                  + [shp(LANES, F32), shp(gm, BF16)],
        compiler_params=_params(2, VMEM_LIMIT),
        name="inproj0",
    )(x, mod, mod, wq, wkv, wu, wv, seg, ng, wsp, bsp)


def _compress_kernel(kc_ref, vc_ref, pos_ref, w1a_ref, w1b_ref, w2_ref, kco_ref, vco_ref):
    n = kc_ref.shape[1] // CMP_STRIDE
    pair = 2 * LANES
    for t, (src, dst) in enumerate(((kc_ref, kco_ref), (vc_ref, vco_ref))):
        ha = jnp.zeros((n, w1a_ref.shape[2]), F32)
        hb = jnp.zeros((n, w1b_ref.shape[2]), F32)
        for l in range(0, CMP_STRIDE, 2):
            x2 = jnp.concatenate([src[0, pl.ds(l, n, stride=CMP_STRIDE), :],
                                  src[0, pl.ds(l + 1, n, stride=CMP_STRIDE), :]], axis=1)
            cols = slice(l * LANES, l * LANES + pair)
            ha = ha + _dot((x2 + pos_ref[t, 0:1, cols]).astype(BF16), w1a_ref[t, cols, :])
            hb = hb + _dot((x2 + pos_ref[t, 1:2, cols]).astype(BF16), w1b_ref[t, cols, :])
        h1 = ha + pltpu.roll(hb, n - 1, axis=0)
        dst[0] = _dot(jax.nn.gelu(h1).astype(BF16), w2_ref[t]).astype(dst.dtype)


def _compress(kc, vc, cmp_pos, cmp_w1, cmp_w2):
    b_, s_, _ = kc.shape
    nch = s_ // CMP_STRIDE
    r = CMP_LEN // CMP_STRIDE
    assert r == 2
    eye = jnp.eye(NSA_KV_HEADS, dtype=cmp_w1.dtype)
    w1 = cmp_w1.reshape(2, r, CMP_STRIDE, HEAD_DIM, CMP_HIDDEN)
    w1 = jnp.einsum("thldc,gk->thlgdkc", w1, eye).reshape(
        2, r, CMP_STRIDE * LANES, NSA_KV_HEADS * CMP_HIDDEN).astype(BF16)
    w2 = jnp.einsum("tcd,gk->tgckd", cmp_w2, eye).reshape(2, NSA_KV_HEADS * CMP_HIDDEN, LANES).astype(BF16)
    pos = jnp.broadcast_to(cmp_pos.reshape(2, r, CMP_STRIDE, 1, HEAD_DIM),
                           (2, r, CMP_STRIDE, NSA_KV_HEADS, HEAD_DIM)).reshape(2, r, CMP_STRIDE * LANES)
    ch_spec = pl.BlockSpec((1, s_, LANES), lambda b: (b, 0, 0))
    out_spec = pl.BlockSpec((1, nch, LANES), lambda b: (b, 0, 0))
    out_shape = jax.ShapeDtypeStruct((b_, nch, LANES), BF16)
    return pl.pallas_call(
        _compress_kernel,
        grid=(b_,),
        in_specs=[ch_spec, ch_spec, _const_spec(pos.shape), _const_spec(w1[:, 0].shape),
                  _const_spec(w1[:, 1].shape), _const_spec(w2.shape)],
        out_specs=[out_spec, out_spec],
        out_shape=[out_shape, out_shape],
        compiler_params=_params(1, VMEM_LIMIT),
        name="compress",
    )(kc, vc, pos, w1[:, 0], w1[:, 1], w2)


def _nsa_kernel(q_ref, gate_ref, kc_ref, vc_ref, ks_ref, vs_ref, kw_ref, vw_ref, cover_ref, o_ref,
                kaug_ref, vst_ref, vwt_ref, vct_ref, s0_ref, s1_ref, c0_ref, c1_ref, w0_ref, w1_ref):
    i = pl.program_id(1)
    start = i * Q_TILE
    s_len = ks_ref.shape[1]
    n_sel = s_len // SEL_LEN
    n_cmp = kc_ref.shape[1]
    rows = NSA_GROUP * Q_TILE
    sub = KV_CHUNK // LANES

    @pl.when(i == 0)
    def _():
        def build(c, carry):
            off = pl.multiple_of(c * KV_CHUNK, KV_CHUNK)
            key_blk = (off + lax.broadcasted_iota(jnp.int32, (KV_CHUNK, LANES), 0)) >> SEL_SHIFT
            ln = lax.broadcasted_iota(jnp.int32, (KV_CHUNK, LANES), 1)
            ks = ks_ref[0, pl.ds(off, KV_CHUNK), :].astype(F32)
            for g in range(NSA_KV_HEADS):
                onehot = (ln - (1 - g) * HALF == key_blk).astype(F32)
                kaug_ref[g, pl.ds(off, KV_CHUNK), :] = jnp.where((ln >> HALF_SHIFT) == g, ks, onehot).astype(BF16)
            for j in range(sub):
                blk = pl.ds(pl.multiple_of(off + j * LANES, LANES), LANES)
                cols = slice(j * LANES, (j + 1) * LANES)
                vs_t = vs_ref[0, blk, :].astype(F32).T.astype(BF16)
                vw_t = vw_ref[0, blk, :].astype(F32).T.astype(BF16)
                for g in range(NSA_KV_HEADS):
                    vst_ref[c, g, 0:HEAD_DIM, cols] = vs_t[g * HALF:(g + 1) * HALF]
                    vwt_ref[c * sub + j, g, 0:HEAD_DIM, :] = vw_t[g * HALF:(g + 1) * HALF]
            return carry
        lax.fori_loop(0, s_len // KV_CHUNK, build, 0)
        for g in range(NSA_KV_HEADS):
            for j in range(n_cmp // LANES):
                cols = slice(j * LANES, (j + 1) * LANES)
                vct_ref[g, 0:HEAD_DIM, cols] = vc_ref[0, cols, :].astype(F32).T[g * HALF:(g + 1) * HALF].astype(BF16)
        ones = lambda ref: jnp.ones(ref.shape[:-2] + (V_ROWS - HEAD_DIM, ref.shape[-1]), BF16)
        vst_ref[:, :, HEAD_DIM:V_ROWS, :] = ones(vst_ref)
        vwt_ref[:, :, HEAD_DIM:V_ROWS, :] = ones(vwt_ref)
        vct_ref[:, HEAD_DIM:V_ROWS, :] = ones(vct_ref)

    q = q_ref[0]
    q_t = jnp.concatenate([q[:, c * LANES:(c + 1) * LANES].T for c in range(q.shape[1] // LANES)], axis=0)
    g_t = gate_ref[0].T
    tpos = start + lax.broadcasted_iota(jnp.int32, (1, Q_TILE), 1)
    j_idx = lax.broadcasted_iota(jnp.int32, (n_sel, 1), 0)
    zeros_half = jnp.zeros((HALF, rows), F32)
    halves = lambda g, own, other: jnp.concatenate([own, other] if g == 0 else [other, own], axis=0).astype(BF16)
    per_head = lambda a: jnp.concatenate([a] * NSA_GROUP, axis=1)
    key_iota = lambda n: lax.broadcasted_iota(jnp.int32, (n, 1), 0)
    strip_max = lambda s: jnp.max(s.reshape(s.shape[0] // 8, 8, rows), axis=0)
    col_max = lambda m8: jnp.max(m8, axis=0, keepdims=True)

    qg = [jnp.concatenate([q_t[(NSA_GROUP * g + r) * HEAD_DIM:(NSA_GROUP * g + r + 1) * HEAD_DIM, :]
                           for r in range(NSA_GROUP)], axis=1) for g in range(NSA_KV_HEADS)]
    q0 = [halves(g, qg[g], zeros_half) for g in range(NSA_KV_HEADS)]
    cbuf, wbuf = (c0_ref, c1_ref), (w0_ref, w1_ref)

    cmp_bias = per_head(jnp.where(key_iota(n_cmp) * CMP_STRIDE + (CMP_LEN - 1) <= tpos, 0.0, NEG_INF))
    has_cmp_key = per_head(tpos >= CMP_LEN - 1)

    def cmp_qk(g):
        s = _dot(kc_ref[0], q0[g]) + cmp_bias
        cbuf[g][...] = s
        return strip_max(s)

    def cmp_pv(g, m8):
        e = jnp.exp2(cbuf[g][...] - col_max(m8))
        nd = _dot(vct_ref[g], e.astype(BF16))
        inv = jnp.where(has_cmp_key, 1.0 / nd[HEAD_DIM:HEAD_DIM + 1], 0.0)
        p = e * inv
        p_sum = p[:, 0:Q_TILE]
        for r in range(1, NSA_GROUP):
            p_sum = p_sum + p[:, r * Q_TILE:(r + 1) * Q_TILE]
        imp = jnp.dot(cover_ref[...], p_sum, precision=lax.Precision.HIGHEST,
                      preferred_element_type=F32)
        return nd[0:HEAD_DIM] * inv, imp

    def select(g, imp):
        cur = tpos >> SEL_SHIFT
        forced = (j_idx == 0) | (j_idx == cur) | (j_idx == cur - 1)
        score = jnp.where(forced, FORCE_SCORE, jnp.where(j_idx * SEL_LEN <= tpos, imp, -FORCE_SCORE))
        strips = [score[b * 8:(b + 1) * 8] for b in range(n_sel // 8)]
        ranks = [jnp.zeros((8, Q_TILE), jnp.int32) for _ in strips]
        for jp in range(n_sel):
            other = score[jp:jp + 1, :]
            for b, strip in enumerate(strips):
                if jp < b * 8:
                    beats = other >= strip
                elif jp >= (b + 1) * 8:
                    beats = other > strip
                else:
                    beats = (other > strip) | ((other == strip) & (jp < j_idx[b * 8:(b + 1) * 8]))
                ranks[b] = ranks[b] + beats.astype(jnp.int32)
        rank = jnp.concatenate(ranks, axis=0)
        bias = jnp.where(rank < min(SEL_TOP_N, n_sel), 0.0, NEG_INF)
        if n_sel < HALF:
            bias = jnp.concatenate([bias, jnp.zeros((HALF - n_sel, Q_TILE), F32)], axis=0)
        return halves(g, qg[g], per_head(bias))

    n_blk = (WINDOW + Q_TILE) // LANES
    wb = jnp.maximum(i * (Q_TILE // LANES) - WINDOW // LANES, 0)
    ws = pl.multiple_of(wb * LANES, LANES)
    dist = tpos - (ws + key_iota(n_blk * LANES))
    win_bias = per_head(jnp.where((dist >= 0) & (dist < WINDOW), 0.0, NEG_INF))

    def win_qk(g):
        s = _dot(kw_ref[0, pl.ds(ws, n_blk * LANES), :], q0[g]) + win_bias
        wbuf[g][...] = s
        return strip_max(s)

    def win_pv(g, m8):
        m = col_max(m8)
        nd = jnp.zeros((V_ROWS, rows), F32)
        for j in range(n_blk):
            blk = slice(j * LANES, (j + 1) * LANES)
            nd = nd + _dot(vwt_ref[wb + j, g], jnp.exp2(wbuf[g][blk, :] - m).astype(BF16))
        return nd[0:HEAD_DIM] * (1.0 / nd[HEAD_DIM:HEAD_DIM + 1])

    def mix(g, o_cmp, o_win):
        per_r = []
        for r in range(NSA_GROUP):
            sl = slice(r * Q_TILE, (r + 1) * Q_TILE)
            c0 = (NSA_GROUP * g + r) * N_BRANCHES
            per_r.append(g_t[c0:c0 + 1] * o_cmp[:, sl] + g_t[c0 + 2:c0 + 3] * o_win[:, sl])
        return per_r

    diag = lax.div(i, KV_CHUNK // Q_TILE)
    qa = [None] * NSA_KV_HEADS
    sbuf = (s0_ref, s1_ref)
    causal_bias = per_head(jnp.where(diag * KV_CHUNK + key_iota(KV_CHUNK) <= tpos, 0.0, NEG_INF))

    def qk_stage(g, c, masked=False):
        off = pl.multiple_of(c * KV_CHUNK, KV_CHUNK)
        s = _dot(kaug_ref[g, pl.ds(off, KV_CHUNK), :], qa[g])
        if masked:
            s = s + causal_bias
        sbuf[g][...] = s
        return strip_max(s)

    def pv_stage(g, c, m8, state):
        m, acc = state
        m_new = jnp.maximum(m, col_max(m8))
        p = jnp.exp2(sbuf[g][...] - m_new).astype(BF16)
        return m_new, jnp.exp2(m - m_new) * acc + _dot(vst_ref[c, g], p)

    cm0 = cmp_qk(0)
    cm1 = cmp_qk(1)
    o_cmp0, imp0 = cmp_pv(0, cm0)
    wm0 = win_qk(0)
    o_cmp1, imp1 = cmp_pv(1, cm1)
    qa[0] = select(0, imp0)
    wm1 = win_qk(1)
    mixed = [mix(0, o_cmp0, win_pv(0, wm0))]
    m8_first = qk_stage(0, 0)
    qa[1] = select(1, imp1)
    mixed.append(mix(1, o_cmp1, win_pv(1, wm1)))

    def sel_chunk(c, carry):
        m8_0, st0, st1 = carry
        m8_1 = qk_stage(1, c)
        st0 = pv_stage(0, c, m8_0, st0)
        m8_0 = qk_stage(0, c + 1)
        st1 = pv_stage(1, c, m8_1, st1)
        return m8_0, st0, st1

    def sel_chunk_pair(t, carry):
        return sel_chunk(2 * t + 1, sel_chunk(2 * t, carry))

    init = (jnp.full((1, rows), NEG_INF, F32), jnp.zeros((V_ROWS, rows), F32))
    n_pairs = lax.div(diag, 2)
    carry = lax.fori_loop(0, n_pairs, sel_chunk_pair, (m8_first, init, init))
    _, st0, st1 = lax.fori_loop(2 * n_pairs, diag, sel_chunk, carry)
    s_d0 = s0_ref[...] + causal_bias
    s0_ref[...] = s_d0
    m8_1 = qk_stage(1, diag, masked=True)
    acc_sel = (pv_stage(0, diag, strip_max(s_d0), st0)[1], pv_stage(1, diag, m8_1, st1)[1])

    heads_out = []
    for g in range(NSA_KV_HEADS):
        acc = acc_sel[g]
        o_sel = acc[0:HEAD_DIM] * (1.0 / acc[HEAD_DIM:HEAD_DIM + 1])
        for r in range(NSA_GROUP):
            sl = slice(r * Q_TILE, (r + 1) * Q_TILE)
            gate = g_t[(NSA_GROUP * g + r) * N_BRANCHES + 1:(NSA_GROUP * g + r) * N_BRANCHES + 2]
            heads_out.append(mixed[g][r] + gate * o_sel[:, sl])
    for c in range(len(heads_out) // 2):
        pair = jnp.concatenate(heads_out[2 * c:2 * c + 2], axis=0)
        o_ref[0, :, c * LANES:(c + 1) * LANES] = pair.T.astype(o_ref.dtype)


def _cover_matrix_t(n_sel, n_cmp_padded):
    jj = np.arange(n_sel)[:, None]
    kk = np.arange(n_cmp_padded)[None, :]
    cover = ((kk * CMP_STRIDE < (jj + 1) * SEL_LEN) & (kk * CMP_STRIDE + CMP_LEN > jj * SEL_LEN)
             & (kk < n_cmp_padded - 1))
    return jnp.asarray(cover, F32)


def _nsa(q, gates, kcc, vcc, ks, vs, kw, vw):
    b_, s_, qd = q.shape
    n_cmp = kcc.shape[1]
    n_sel = s_ // SEL_LEN
    assert Q_TILE % LANES == 0 and KV_CHUNK % Q_TILE == 0 and n_sel <= HALF and n_sel % 8 == 0
    assert n_cmp % LANES == 0 and s_ % KV_CHUNK == 0 and s_ >= WINDOW + Q_TILE and WINDOW % LANES == 0
    cover = _cover_matrix_t(n_sel, n_cmp)
    tok = lambda w: pl.BlockSpec((1, Q_TILE, w), lambda b, i: (b, i, 0))
    seq = lambda n: pl.BlockSpec((1, n, LANES), lambda b, i: (b, 0, 0))
    return pl.pallas_call(
        _nsa_kernel,
        grid=(b_, s_ // Q_TILE),
        in_specs=[tok(qd), tok(LANES), seq(n_cmp), seq(n_cmp), seq(s_), seq(s_), seq(s_), seq(s_),
                  _const_spec(cover.shape)],
        out_specs=tok(qd),
        out_shape=jax.ShapeDtypeStruct((b_, s_, qd), BF16),
        scratch_shapes=[pltpu.VMEM((NSA_KV_HEADS, s_, LANES), BF16),
                        pltpu.VMEM((s_ // KV_CHUNK, NSA_KV_HEADS, V_ROWS, KV_CHUNK), BF16),
                        pltpu.VMEM((s_ // LANES, NSA_KV_HEADS, V_ROWS, LANES), BF16),
                        pltpu.VMEM((NSA_KV_HEADS, V_ROWS, n_cmp), BF16),
                        pltpu.VMEM((KV_CHUNK, NSA_GROUP * Q_TILE), F32),
                        pltpu.VMEM((KV_CHUNK, NSA_GROUP * Q_TILE), F32),
                        pltpu.VMEM((n_cmp, NSA_GROUP * Q_TILE), F32),
                        pltpu.VMEM((n_cmp, NSA_GROUP * Q_TILE), F32),
                        pltpu.VMEM((WINDOW + Q_TILE, NSA_GROUP * Q_TILE), F32),
                        pltpu.VMEM((WINDOW + Q_TILE, NSA_GROUP * Q_TILE), F32)],
        compiler_params=_params(2, VMEM_LIMIT),
        name="nsa",
    )(q, gates, kcc, vcc, ks, vs, kw, vw, cover)


def _swiglu_steps(y, shift, scale, gate, w_in_ref, w_out_ref):
    h = (y * (1.0 + scale) + shift).astype(BF16)
    hid = w_out_ref.shape[0]
    acc = jnp.zeros(y.shape, F32)
    for c in range(hid // FFN_CHUNK):
        lo = c * FFN_CHUNK
        g_ = _dot(h, w_in_ref[:, lo:lo + FFN_CHUNK])
        u_ = _dot(h, w_in_ref[:, hid + lo:hid + lo + FFN_CHUNK])
        acc = acc + _dot((jax.nn.silu(g_) * u_).astype(BF16), w_out_ref[lo:lo + FFN_CHUNK, :])
        yield None
    yield ALPHA * y + (1.0 + gate) * acc


def _mix0_kernel(x_ref, on_ref, og_ref, gate1_ref, wo_ref, ln1g_ref, ln1b_ref,
                 shift2_ref, scale2_ref, gate2_ref, fin_ref, fout_ref, ln2g_ref, ln2b_ref, o_ref):
    half = on_ref.shape[2]
    tm = x_ref.shape[1] // 2
    ra, rb = slice(0, tm), slice(tm, 2 * tm)
    w_out = lambda r: _dot(on_ref[0, r, :], wo_ref[0:half, :]) + _dot(og_ref[0, r, :], wo_ref[half:2 * half, :])
    ln1 = lambda r, out: _layer_norm(ALPHA * x_ref[0, r, :] + (1.0 + gate1_ref[0]) * out, ln1g_ref[...], ln1b_ref[...])
    ffn = lambda y: _swiglu_steps(y, shift2_ref[0], scale2_ref[0], gate2_ref[0], fin_ref, fout_ref)
    out_a = w_out(ra)
    out_b = w_out(rb)
    steps_a = ffn(ln1(ra, out_a))
    next(steps_a)
    steps_b = ffn(ln1(rb, out_b))
    res_a = [v for v in steps_a if v is not None][0]
    next(steps_b)
    o_ref[0, ra, :] = _layer_norm(res_a, ln2g_ref[...], ln2b_ref[...])
    res_b = [v for v in steps_b if v is not None][0]
    o_ref[0, rb, :] = _layer_norm(res_b, ln2g_ref[...], ln2b_ref[...])


def _mix0(x, o_nsa, o_gmlp, mod, w_out, ln_g, ln_b, ffn_w_in, ffn_w_out):
    b_, s_, d = x.shape
    tm = 2 * ROW_TILE
    wo = w_out.astype(BF16)
    fin = ffn_w_in.astype(BF16)
    fout = ffn_w_out.astype(BF16)
    row = lambda v: v.reshape(1, d)
    tok = lambda w: pl.BlockSpec((1, tm, w), lambda b, i: (b, i, 0))
    return pl.pallas_call(
        _mix0_kernel,
        grid=(b_, s_ // tm),
        in_specs=[tok(d), tok(o_nsa.shape[2]), tok(o_gmlp.shape[2]), _mod_spec(0, 2, b_, d),
                  _const_spec(wo.shape), _const_spec((1, d)), _const_spec((1, d)),
                  _mod_spec(1, 0, b_, d), _mod_spec(1, 1, b_, d), _mod_spec(1, 2, b_, d),
                  _const_spec(fin.shape), _const_spec(fout.shape), _const_spec((1, d)), _const_spec((1, d))],
        out_specs=tok(d),
        out_shape=jax.ShapeDtypeStruct(x.shape, F32),
        compiler_params=_params(2, VMEM_LIMIT),
        name="mix0",
    )(x, o_nsa, o_gmlp, mod, wo, row(ln_g[0]), row(ln_b[0]), mod, mod, mod, fin, fout, row(ln_g[1]), row(ln_b[1]))


def _layer1_kernel(x_ref, shift1_ref, scale1_ref, gate1_ref, win_ref, cw_ref, wo_ref, ln1g_ref, ln1b_ref,
                   shift2_ref, scale2_ref, gate2_ref, fin_ref, fout_ref, ln2g_ref, ln2b_ref, o_ref, carry_ref):
    i = pl.program_id(1)
    tm, d = x_ref.shape[1] // 2, x_ref.shape[2]
    halo = carry_ref.shape[0]
    ra, rb = slice(0, tm), slice(tm, 2 * tm)

    @pl.when(i == 0)
    def _():
        carry_ref[...] = jnp.zeros(carry_ref.shape, F32)

    def project(r):
        h = (x_ref[0, r, :] * (1.0 + scale1_ref[0]) + shift1_ref[0]).astype(BF16)
        return _dot(h, win_ref[:, 0:d]), _dot(h, win_ref[:, d:2 * d]) * _dot(h, win_ref[:, 2 * d:3 * d])

    def conv_out(b_gate, cz, prev):
        row = lax.broadcasted_iota(jnp.int32, (tm, 1), 0)
        back1 = jnp.where(row == 0, prev[halo - 1:halo], pltpu.roll(cz, 1, axis=0))
        back2 = jnp.where(row == 0, prev[halo - 2:halo - 1],
                          jnp.where(row == 1, prev[halo - 1:halo], pltpu.roll(cz, 2, axis=0)))
        cw = cw_ref[...]
        y = cw[0:1] * back2 + cw[1:2] * back1 + cw[2:3] * cz
        return _dot((b_gate * y).astype(BF16), wo_ref[...])

    ln1 = lambda r, out: _layer_norm(ALPHA * x_ref[0, r, :] + (1.0 + gate1_ref[0]) * out,
                                     ln1g_ref[...], ln1b_ref[...])
    ffn = lambda y: _swiglu_steps(y, shift2_ref[0], scale2_ref[0], gate2_ref[0], fin_ref, fout_ref)
    bg_a, cz_a = project(ra)
    bg_b, cz_b = project(rb)
    out_a = conv_out(bg_a, cz_a, carry_ref[...])
    out_b = conv_out(bg_b, cz_b, cz_a[tm - halo:tm])
    carry_ref[...] = cz_b[tm - halo:tm]
    steps_a = ffn(ln1(ra, out_a))
    next(steps_a)
    steps_b = ffn(ln1(rb, out_b))
    res_a = [v for v in steps_a if v is not None][0]
    next(steps_b)
    o_ref[0, ra, :] = _layer_norm(res_a, ln2g_ref[...], ln2b_ref[...])
    res_b = [v for v in steps_b if v is not None][0]
    o_ref[0, rb, :] = _layer_norm(res_b, ln2g_ref[...], ln2b_ref[...])


def _layer1(x, mod, w_in, conv_w, w_out, ln_g, ln_b, ffn_w_in, ffn_w_out):
    b_, s_, d = x.shape
    tm = 2 * ROW_TILE
    assert conv_w.shape[0] == CONV_WIDTH == 3
    win = w_in.astype(BF16)
    wo = w_out.astype(BF16)
    fin = ffn_w_in.astype(BF16)
    fout = ffn_w_out.astype(BF16)
    row = lambda v: v.reshape(1, d)
    tok = pl.BlockSpec((1, tm, d), lambda b, i: (b, i, 0))
    return pl.pallas_call(
        _layer1_kernel,
        grid=(b_, s_ // tm),
        in_specs=[tok, _mod_spec(2, 0, b_, d), _mod_spec(2, 1, b_, d), _mod_spec(2, 2, b_, d),
                  _const_spec(win.shape), _const_spec(conv_w.shape), _const_spec(wo.shape),
                  _const_spec((1, d)), _const_spec((1, d)),
                  _mod_spec(3, 0, b_, d), _mod_spec(3, 1, b_, d), _mod_spec(3, 2, b_, d),
                  _const_spec(fin.shape), _const_spec(fout.shape), _const_spec((1, d)), _const_spec((1, d))],
        out_specs=tok,
        out_shape=jax.ShapeDtypeStruct(x.shape, F32),
        scratch_shapes=[pltpu.VMEM((8, d), F32)],
        compiler_params=_params(2, VMEM_LIMIT),
        name="layer1",
    )(x, mod, mod, mod, win, conv_w, wo, row(ln_g[0]), row(ln_b[0]), mod, mod, mod, fin, fout,
      row(ln_g[1]), row(ln_b[1]))


def kernel(x, c, ada_w, ada_b, ln_g, ln_b, even_w_in, even_cmp_pos, even_cmp_w1, even_cmp_w2, even_gmlp_norm_g,
           even_gmlp_ws, even_gmlp_bs, even_w_out, odd_w_in, odd_conv_w, odd_w_out, ffn_w_in, ffn_w_out):
    assert ada_w.shape[0] == DEPTH == 2
    mod = _ada_modulation(c, ada_w, ada_b)
    q, kc, vc, ks, vs, kw, vw, gates, o_gmlp = _inproj0(
        x, mod, even_w_in[0], even_gmlp_norm_g[0], even_gmlp_ws[0], even_gmlp_bs[0])
    kcc, vcc = _compress(kc, vc, even_cmp_pos[0], even_cmp_w1[0], even_cmp_w2[0])
    o_nsa = _nsa(q, gates, kcc, vcc, ks, vs, kw, vw)
    x = _mix0(x, o_nsa, o_gmlp, mod, even_w_out[0], ln_g[0], ln_b[0], ffn_w_in[0], ffn_w_out[0])
    return _layer1(x, mod, odd_w_in[0], odd_conv_w[0], odd_w_out[0], ln_g[1], ln_b[1], ffn_w_in[1], ffn_w_out[1])
```
